```python
import jax, jax.numpy as jnp
from jax import lax
import numpy as np

D_MODEL = 1024
BATCH = 32
SEQ = 256
DEPTH = 1
DEC_BATCH = 8
DEC_SEQ = 4096
PAST_LEN = 256

GRID_W = 64
H_A = 4
DK_A = 128
DV_A = 128
CONV_W = 5
CHUNK = 64
H_B = 8
KV_B = 2
GROUP_B = H_B // KV_B
HD_B = 64
WINDOW = 128
BLOCK_B = 128
ROPE_BASE = 10000.0
ROPE_AXIS_PAIRS = HD_B // 4
D_FF = 2816
HALF_STEP = 0.5
N_MOD = 9
EPS = 1e-6
QKV_A = 2 * H_A * DK_A + H_A * DV_A
Z_A = H_A * DV_A
DEC_A = 2 * H_A
BETA_A = 2 * H_A
Q_B = H_B * HD_B
K_B = KV_B * HD_B
V_B = KV_B * HD_B
GATES = 2 * D_MODEL
SPLIT_SIZES = (QKV_A, Z_A, DEC_A, BETA_A, Q_B, K_B, V_B, GATES)
IN_WIDTH = QKV_A + Z_A + DEC_A + BETA_A + Q_B + K_B + V_B + GATES

kernel_name = 'hybrid_deltanet_swa_prefix_dit_step'


def rms_norm(x, w):
    xf = x.astype(jnp.float32)
    y = xf * lax.rsqrt(jnp.mean(xf * xf, axis=-1, keepdims=True) + EPS)
    return (y * w.astype(jnp.float32)).astype(x.dtype)


def l2norm(x):
    return x * lax.rsqrt(jnp.sum(x * x, axis=-1, keepdims=True) + EPS)


def modulate(x, shift, scale):
    return x * (1 + scale) + shift


def swiglu(h, w13, w2):
    a, b = jnp.split(h @ w13, 2, axis=-1)
    return (jax.nn.silu(a) * b) @ w2


def ffn_sublayer(x, norm_w, shift, scale, gate, w13, w2):
    return x + HALF_STEP * gate * swiglu(modulate(rms_norm(x, norm_w), shift, scale), w13, w2)


def split_mixer_inputs(p):
    points = [int(v) for v in np.cumsum(SPLIT_SIZES)[:-1]]
    return jnp.split(p, points, axis=-1)


def short_conv(x, w):
    C = x.shape[-1]
    y = lax.conv_general_dilated(x, w[:, None, :].astype(x.dtype), window_strides=(1,),
                                 padding=((CONV_W // 2, CONV_W // 2),),
                                 dimension_numbers=('NWC', 'WIO', 'NWC'), feature_group_count=C)
    return jax.nn.silu(y)


def chunk_gated_delta(q, k, v, g, beta, s0):
    B_, T, H, _ = q.shape
    DV = v.shape[-1]
    n = T // CHUNK

    def blocks(t):
        t = t.reshape((B_, n, CHUNK, H) + t.shape[3:])
        return jnp.moveaxis(t, (1, 3), (0, 2))

    qc, kc, vc, bc = blocks(q), blocks(k), blocks(v), blocks(beta)
    gc = jnp.cumsum(blocks(g), axis=-1)
    idx = jnp.arange(CHUNK)
    causal = idx[:, None] >= idx[None, :]
    strict = idx[:, None] > idx[None, :]
    decay = jnp.exp(jnp.where(causal, gc[..., :, None] - gc[..., None, :], -jnp.inf))
    kb = kc * bc[..., None]
    vb = vc * bc[..., None]
    lower = jnp.where(strict, jnp.einsum('nbhid,nbhjd->nbhij', kb, kc) * decay, 0.0)
    eye = jnp.eye(CHUNK, dtype=jnp.float32)
    tmat = lax.linalg.triangular_solve(eye + lower, jnp.broadcast_to(eye, lower.shape),
                                       left_side=True, lower=True, unit_diagonal=True)
    u = tmat @ vb
    w = tmat @ (kb * jnp.exp(gc)[..., None])
    a_intra = jnp.einsum('nbhid,nbhjd->nbhij', qc, kc) * decay

    def step(S, xs):
        q_i, k_i, u_i, w_i, a_i, g_i = xs
        v_new = u_i - w_i @ S
        o_i = (q_i * jnp.exp(g_i)[..., None]) @ S + a_i @ v_new
        g_last = g_i[..., -1:]
        S = S * jnp.exp(g_last)[..., None] + jnp.einsum(
            'bhck,bhcv->bhkv', k_i * jnp.exp(g_last - g_i)[..., None], v_new)
        return S, o_i

    s_final, o = lax.scan(step, s0.astype(jnp.float32), (qc, kc, u, w, a_intra, gc))
    o = jnp.moveaxis(o, (0, 2), (1, 3)).reshape(B_, T, H, DV)
    return o, s_final


def mixer_a(qkv_raw, z_raw, dec_raw, beta_raw, conv_w, a_log, dt_bias, onorm, s0_f, s0_b):
    B_, T = qkv_raw.shape[:2]
    qkv = short_conv(qkv_raw, conv_w).astype(jnp.float32)
    q, k, v = jnp.split(qkv, [H_A * DK_A, 2 * H_A * DK_A], axis=-1)
    q = l2norm(q.reshape(B_, T, H_A, DK_A)) * DK_A ** -0.5
    k = l2norm(k.reshape(B_, T, H_A, DK_A))
    v = v.reshape(B_, T, H_A, DV_A)
    g = -jnp.exp(a_log.astype(jnp.float32)) * jax.nn.softplus(
        dec_raw.astype(jnp.float32).reshape(B_, T, 2, H_A) + dt_bias.astype(jnp.float32))
    beta = jax.nn.sigmoid(beta_raw.astype(jnp.float32).reshape(B_, T, 2, H_A))
    o_f, s_f = chunk_gated_delta(q, k, v, g[:, :, 0], beta[:, :, 0], s0_f)
    rev = lambda t: jnp.flip(t, axis=1)
    o_b, s_b = chunk_gated_delta(rev(q), rev(k), rev(v), rev(g[:, :, 1]), rev(beta[:, :, 1]), s0_b)
    o = o_f + rev(o_b)
    y = rms_norm(o, onorm) * jax.nn.silu(z_raw.astype(jnp.float32).reshape(B_, T, H_A, DV_A))
    return y.reshape(B_, T, H_A * DV_A).astype(qkv_raw.dtype), s_f, s_b


def axial_rope_tables(rows):
    row = jnp.repeat(jnp.arange(rows, dtype=jnp.float32), GRID_W)
    col = jnp.tile(jnp.arange(GRID_W, dtype=jnp.float32), rows)
    inv = jnp.power(ROPE_BASE, -jnp.arange(ROPE_AXIS_PAIRS, dtype=jnp.float32) / ROPE_AXIS_PAIRS)
    ang = jnp.concatenate([row[:, None] * inv, col[:, None] * inv], axis=-1)
    return jnp.cos(ang), jnp.sin(ang)


def apply_rope(x, cos, sin):
    xf = x.astype(jnp.float32)
    x1, x2 = jnp.split(xf, 2, axis=-1)
    c = cos[None, :, None, :]
    s = sin[None, :, None, :]
    return jnp.concatenate([x1 * c - x2 * s, x2 * c + x1 * s], axis=-1).astype(x.dtype)


def context_attention(q, k, v, sink):
    B_, S = q.shape[:2]
    qg = q.reshape(B_, S, KV_B, GROUP_B, HD_B)
    s = jnp.einsum('bqgrd,bkgd->bgrqk', qg, k).astype(jnp.float32) * HD_B ** -0.5
    s_snk = jnp.broadcast_to(sink.astype(jnp.float32).reshape(KV_B, GROUP_B, 1, 1), s.shape[:-1] + (1,))
    p = jax.nn.softmax(jnp.concatenate([s, s_snk], axis=-1), axis=-1)[..., :S].astype(v.dtype)
    o = jnp.einsum('bgrqk,bkgd->bqgrd', p, v)
    return o.reshape(B_, S, H_B * HD_B)


def window_attention(q, k, v, k_ctx, v_ctx, sink):
    B_, T = q.shape[:2]
    n = T // BLOCK_B
    scale = HD_B ** -0.5
    qb = jnp.moveaxis(q.reshape(B_, n, BLOCK_B, KV_B, GROUP_B, HD_B), 1, 0)

    def windows(t):
        tp = jnp.pad(t, ((0, 0), (BLOCK_B, BLOCK_B), (0, 0), (0, 0))).reshape(B_, n + 2, BLOCK_B, KV_B, HD_B)
        w = jnp.concatenate([tp[:, :-2], tp[:, 1:-1], tp[:, 2:]], axis=2)
        return jnp.moveaxis(w, 1, 0)

    kw, vw = windows(k), windows(v)
    blk = jnp.arange(n)[:, None]
    q_pos = blk * BLOCK_B + jnp.arange(BLOCK_B)[None, :]
    k_pos = (blk - 1) * BLOCK_B + jnp.arange(3 * BLOCK_B)[None, :]
    valid = ((jnp.abs(q_pos[:, :, None] - k_pos[:, None, :]) <= WINDOW)
             & (k_pos[:, None, :] >= 0) & (k_pos[:, None, :] < T))
    sink_logit = sink.astype(jnp.float32).reshape(KV_B, GROUP_B, 1, 1)
    neg = jnp.finfo(jnp.float32).min
    n_loc = 3 * BLOCK_B
    n_ctx = k_ctx.shape[1]

    def one_block(args):
        q_i, k_i, v_i, m_i = args
        s_loc = jnp.einsum('bqgrd,bkgd->bgrqk', q_i, k_i).astype(jnp.float32) * scale
        s_loc = jnp.where(m_i[None, None, None], s_loc, neg)
        s_ctx = jnp.einsum('bqgrd,bkgd->bgrqk', q_i, k_ctx).astype(jnp.float32) * scale
        s_snk = jnp.broadcast_to(sink_logit, s_loc.shape[:-1] + (1,))
        p = jax.nn.softmax(jnp.concatenate([s_loc, s_ctx, s_snk], axis=-1), axis=-1).astype(v_i.dtype)
        return (jnp.einsum('bgrqk,bkgd->bqgrd', p[..., :n_loc], v_i)
                + jnp.einsum('bgrqk,bkgd->bqgrd', p[..., n_loc:n_loc + n_ctx], v_ctx))

    o = lax.map(one_block, (qb, kw, vw, valid))
    return jnp.moveaxis(o, 0, 1).reshape(B_, T, H_B * HD_B)


def merge_branches(y_a, y_b, gate_raw, w_oa, w_ob, w_out):
    g_a, g_b = jnp.split(jax.nn.sigmoid(gate_raw), 2, axis=-1)
    return (g_a * (y_a @ w_oa) + g_b * (y_b @ w_ob)) @ w_out


def setup_inputs(seed: int = 0) -> dict:
    key = jax.random.key(seed)
    ks = jax.random.split(key, 32)
    nrm = lambda k, shape, s: jax.random.normal(k, shape, jnp.float32) * s
    dt = jnp.exp(jax.random.uniform(ks[17], (DEPTH, 2, H_A), jnp.float32, jnp.log(1e-3), jnp.log(1e-1)))
    return {
        'x_prompt': nrm(ks[0], (BATCH, SEQ, D_MODEL), 1.0),
        'x_sample': nrm(ks[1], (DEC_BATCH, DEC_SEQ, D_MODEL), 1.0),
        'state_delta_fwd': nrm(ks[2], (DEC_BATCH, DEPTH, H_A, DK_A, DV_A), 0.1),
        'state_delta_bwd': nrm(ks[3], (DEC_BATCH, DEPTH, H_A, DK_A, DV_A), 0.1),
        'cache_k': nrm(ks[4], (DEC_BATCH, DEPTH, PAST_LEN, KV_B, HD_B), 1.0),
        'cache_v': nrm(ks[5], (DEC_BATCH, DEPTH, PAST_LEN, KV_B, HD_B), 1.0),
        'c': nrm(ks[6], (DEC_BATCH, D_MODEL), 1.0),
        'c_ctx': nrm(ks[7], (D_MODEL,), 1.0),
        'ada_w': nrm(ks[8], (DEPTH, D_MODEL, N_MOD * D_MODEL), 0.5 * D_MODEL ** -0.5),
        'ada_b': nrm(ks[9], (DEPTH, N_MOD * D_MODEL), 0.01),
        'norm_ffn1': 1.0 + nrm(ks[10], (DEPTH, D_MODEL), 0.01),
        'ffn1_w13': nrm(ks[11], (DEPTH, D_MODEL, 2 * D_FF), D_MODEL ** -0.5),
        'ffn1_w2': nrm(ks[12], (DEPTH, D_FF, D_MODEL), D_FF ** -0.5),
        'norm_mix': 1.0 + nrm(ks[13], (DEPTH, D_MODEL), 0.01),
        'w_in': nrm(ks[14], (DEPTH, D_MODEL, IN_WIDTH), D_MODEL ** -0.5),
        'conv_w': nrm(ks[15], (DEPTH, CONV_W, QKV_A), CONV_W ** -0.5),
        'a_log': jnp.log(jax.random.uniform(ks[16], (DEPTH, 2, H_A), jnp.float32, 1.0, 16.0)),
        'dt_bias': dt + jnp.log(-jnp.expm1(-dt)),
        'onorm_a': 1.0 + nrm(ks[18], (DEPTH, DV_A), 0.01),
        'w_oa': nrm(ks[19], (DEPTH, H_A * DV_A, D_MODEL), (H_A * DV_A) ** -0.5),
        'w_ob': nrm(ks[20], (DEPTH, H_B * HD_B, D_MODEL), (H_B * HD_B) ** -0.5),
        'w_out': nrm(ks[21], (DEPTH, D_MODEL, D_MODEL), D_MODEL ** -0.5),
        'sink': nrm(ks[22], (DEPTH, H_B), 0.5),
        'norm_ffn2': 1.0 + nrm(ks[23], (DEPTH, D_MODEL), 0.01),
        'ffn2_w13': nrm(ks[24], (DEPTH, D_MODEL, 2 * D_FF), D_MODEL ** -0.5),
        'ffn2_w2': nrm(ks[25], (DEPTH, D_FF, D_MODEL), D_FF ** -0.5),
        'norm_final': 1.0 + nrm(ks[26], (D_MODEL,), 0.01),
    }


def reference(x_prompt, x_sample, state_delta_fwd, state_delta_bwd, cache_k, cache_v, c, c_ctx,
              ada_w, ada_b, norm_ffn1, ffn1_w13, ffn1_w2, norm_mix, w_in, conv_w, a_log, dt_bias,
              onorm_a, w_oa, w_ob, w_out, sink, norm_ffn2, ffn2_w13, ffn2_w2, norm_final):
    B_p, S_p = x_prompt.shape[:2]
    B_s, T_s = x_sample.shape[:2]
    rows = T_s // GRID_W
    cos, sin = axial_rope_tables(rows)
    xp, xs = x_prompt, x_sample
    st_f, st_b, ck, cv = [], [], [], []
    for l in range(DEPTH):
        mod_p = (jax.nn.silu(c_ctx) @ ada_w[l] + ada_b[l])[None, None, :]
        mod_s = (jax.nn.silu(c) @ ada_w[l] + ada_b[l])[:, None, :]
        sh1p, sc1p, g1p, sh2p, sc2p, g2p, sh3p, sc3p, g3p = jnp.split(mod_p, N_MOD, axis=-1)
        sh1s, sc1s, g1s, sh2s, sc2s, g2s, sh3s, sc3s, g3s = jnp.split(mod_s, N_MOD, axis=-1)

        xp = ffn_sublayer(xp, norm_ffn1[l], sh1p, sc1p, g1p, ffn1_w13[l], ffn1_w2[l])
        hp = modulate(rms_norm(xp, norm_mix[l]), sh2p, sc2p)
        qkv_a, z_a, dec_a, beta_a, q_b, k_b, v_b, gate_raw = split_mixer_inputs(hp @ w_in[l])
        zero_state = jnp.zeros((B_p, H_A, DK_A, DV_A), jnp.float32)
        y_a, s_f, s_b = mixer_a(qkv_a, z_a, dec_a, beta_a, conv_w[l], a_log[l], dt_bias[l], onorm_a[l],
                                zero_state, zero_state)
        k_p = k_b.reshape(B_p, S_p, KV_B, HD_B)
        v_p = v_b.reshape(B_p, S_p, KV_B, HD_B)
        y_b = context_attention(q_b.reshape(B_p, S_p, H_B, HD_B), k_p, v_p, sink[l])
        xp = xp + g2p * merge_branches(y_a, y_b, gate_raw, w_oa[l], w_ob[l], w_out[l])
        xp = ffn_sublayer(xp, norm_ffn2[l], sh3p, sc3p, g3p, ffn2_w13[l], ffn2_w2[l])
        st_f.append(s_f)
        st_b.append(s_b)
        ck.append(k_p)
        cv.append(v_p)

        xs = ffn_sublayer(xs, norm_ffn1[l], sh1s, sc1s, g1s, ffn1_w13[l], ffn1_w2[l])
        hs = modulate(rms_norm(xs, norm_mix[l]), sh2s, sc2s)
        qkv_a, z_a, dec_a, beta_a, q_b, k_b, v_b, gate_raw = split_mixer_inputs(hs @ w_in[l])
        y_a, _, _ = mixer_a(qkv_a, z_a, dec_a, beta_a, conv_w[l], a_log[l], dt_bias[l], onorm_a[l],
                            state_delta_fwd[:, l], state_delta_bwd[:, l])
        q_s = apply_rope(q_b.reshape(B_s, T_s, H_B, HD_B), cos, sin)
        k_s = apply_rope(k_b.reshape(B_s, T_s, KV_B, HD_B), cos, sin)
        y_b = window_attention(q_s, k_s, v_b.reshape(B_s, T_s, KV_B, HD_B), cache_k[:, l], cache_v[:, l], sink[l])
        xs = xs + g2s * merge_branches(y_a, y_b, gate_raw, w_oa[l], w_ob[l], w_out[l])
        xs = ffn_sublayer(xs, norm_ffn2[l], sh3s, sc3s, g3s, ffn2_w13[l], ffn2_w2[l])

    y_prompt = rms_norm(xp, norm_final)
    y_sample = rms_norm(xs, norm_final)
    new_state_delta_fwd = jnp.stack(st_f, axis=1)
    new_state_delta_bwd = jnp.stack(st_b, axis=1)
    new_cache_k = jnp.stack(ck, axis=1)
    new_cache_v = jnp.stack(cv, axis=1)
    return (y_prompt, y_sample, new_state_delta_fwd, new_state_delta_bwd, new_cache_k, new_cache_v)
```

```python
import functools

import jax
import jax.numpy as jnp
import numpy as np
from jax import lax
from jax.experimental import pallas as pl
from jax.experimental.pallas import tpu as pltpu

D_MODEL = 1024
GRID_W = 64
H_A = 4
DK_A = 128
DV_A = 128
CONV_W = 5
H_B = 8
KV_B = 2
HD_B = 64
WINDOW = 128
ROPE_BASE = 10000.0
ROPE_AXIS_PAIRS = HD_B // 4
D_FF = 2816
HALF_STEP = 0.5
N_MOD = 9
EPS = 1e-6

QKV_A = 2 * H_A * DK_A + H_A * DV_A
Z_A = H_A * DV_A
DEC_A = 2 * H_A
BETA_A = 2 * H_A
Q_B = H_B * HD_B
K_B = KV_B * HD_B
V_B = KV_B * HD_B
GATES = 2 * D_MODEL

LANES = 128
SUBLANES = 8
CHUNK = 128
NEG = -1e30

BF = jnp.bfloat16
F32 = jnp.float32
VMEM_BIG = 56 * 1024 * 1024


def _dot(a, b):
    return jnp.dot(a.astype(BF), b.astype(BF), preferred_element_type=F32)


def _dot_nt(a, b):
    return lax.dot_general(a.astype(BF), b.astype(BF), (((1,), (1,)), ((), ())),
                           preferred_element_type=F32)


def _sigmoid(x):
    return 1.0 / (1.0 + jnp.exp(-x))


def _silu(x):
    return x * _sigmoid(x)


def _rms(x, w):
    return x * lax.rsqrt(jnp.mean(x * x, axis=-1, keepdims=True) + EPS) * w


def _const_spec(shape):
    nd = len(shape)
    return pl.BlockSpec(shape, lambda *_: (0,) * nd, pipeline_mode=pl.Buffered(1))


def _mod_spec(kind, tm, tokens_per_row, row0):
    return pl.BlockSpec((1, 1, 1, D_MODEL),
                        lambda i: (row0 + (i * tm) // tokens_per_row, kind, 0, 0))


def _mod_kernel(c_ref, w_ref, b_ref, o_ref):
    o_ref[...] = _dot(_silu(c_ref[...]), w_ref[...]) + b_ref[...]


def _modulation(cvec, ada_w, ada_b):
    rows = cvec.shape[0]
    n = N_MOD * D_MODEL
    bn = 1152
    return pl.pallas_call(
        _mod_kernel,
        grid=(n // bn,),
        in_specs=[pl.BlockSpec((rows, D_MODEL), lambda j: (0, 0)),
                  pl.BlockSpec((D_MODEL, bn), lambda j: (0, j)),
                  pl.BlockSpec((1, bn), lambda j: (0, j))],
        out_specs=pl.BlockSpec((rows, bn), lambda j: (0, j)),
        out_shape=jax.ShapeDtypeStruct((rows, n), F32),
        name="modulation",
    )(cvec, ada_w, ada_b.reshape(1, n))


def _ffn_kernel(x_ref, sh_ref, sc_ref, g_ref, nw_ref, w13_ref, w2_ref, fw_ref, o_ref, *, ff_chunk, final):
    x = x_ref[...]
    h = (_rms(x, nw_ref[...]) * (1.0 + sc_ref[0, 0]) + sh_ref[0, 0]).astype(BF)
    acc = jnp.zeros(x.shape, F32)
    for j in range(D_FF // ff_chunk):
        a = jnp.dot(h, w13_ref[:, j * ff_chunk:(j + 1) * ff_chunk], preferred_element_type=F32)
        b = jnp.dot(h, w13_ref[:, D_FF + j * ff_chunk:D_FF + (j + 1) * ff_chunk],
                    preferred_element_type=F32)
        t = (_silu(a) * b).astype(BF)
        acc = acc + jnp.dot(t, w2_ref[j * ff_chunk:(j + 1) * ff_chunk, :], preferred_element_type=F32)
    y = x + (HALF_STEP * g_ref[0, 0]) * acc
    if final:
        y = _rms(y, fw_ref[...])
    o_ref[...] = y


def _ffn(x, mod4, kinds, tokens_per_row, row0, norm_w, w13, w2, final_w, final):
    n = x.shape[0]
    tm = 512
    tok = pl.BlockSpec((tm, D_MODEL), lambda i: (i, 0))
    return pl.pallas_call(
        functools.partial(_ffn_kernel, ff_chunk=1408, final=final),
        grid=(n // tm,),
        in_specs=[tok,
                  _mod_spec(kinds[0], tm, tokens_per_row, row0),
                  _mod_spec(kinds[1], tm, tokens_per_row, row0),
                  _mod_spec(kinds[2], tm, tokens_per_row, row0),
                  _const_spec((1, D_MODEL)),
                  _const_spec((D_MODEL, 2 * D_FF)),
                  _const_spec((D_FF, D_MODEL)),
                  _const_spec((1, D_MODEL))],
        out_specs=tok,
        out_shape=jax.ShapeDtypeStruct((n, D_MODEL), F32),
        compiler_params=pltpu.CompilerParams(dimension_semantics=("parallel",),
                                             vmem_limit_bytes=VMEM_BIG),
        name="ffn_final" if final else "ffn",
    )(x, mod4, mod4, mod4, norm_w, w13, w2, final_w)


IN_SPLITS = (QKV_A, Z_A, Q_B, K_B, V_B, GATES, LANES)
IN_WIDTH_P = sum(IN_SPLITS)


def _swap_halves(x):
    w = x.shape[-1]
    lane = lax.broadcasted_iota(jnp.int32, x.shape, 1)
    lower = (lane % HD_B) < (HD_B // 2)
    return jnp.where(lower, pltpu.roll(x, w - HD_B // 2, 1), pltpu.roll(x, HD_B // 2, 1))


def _inproj_kernel(x_ref, sh_ref, sc_ref, nw_ref, w_ref, cos_ref, sin_ref,
                   qkv_ref, z_ref, qb_ref, kb_ref, vb_ref, gate_ref, db_ref, *, rope):
    h = (_rms(x_ref[...], nw_ref[...]) * (1.0 + sc_ref[0, 0]) + sh_ref[0, 0]).astype(BF)
    outs = (qkv_ref, z_ref, qb_ref, kb_ref, vb_ref, gate_ref, db_ref)
    off = 0
    for width, ref in zip(IN_SPLITS, outs):
        p = jnp.dot(h, w_ref[:, off:off + width], preferred_element_type=F32)
        if rope and (ref is qb_ref or ref is kb_ref):
            reps = width // LANES
            c = jnp.concatenate([cos_ref[...]] * reps, axis=1) if reps > 1 else cos_ref[...]
            s = jnp.concatenate([sin_ref[...]] * reps, axis=1) if reps > 1 else sin_ref[...]
            p = p * c + _swap_halves(p) * s
        ref[...] = p
        off += width


def _inproj(x, mod4, tokens_per_row, row0, norm_w, w_in_p, cos_t, sin_t, seq, rope):
    n = x.shape[0]
    tm = 256
    tiles_per_seq = seq // tm
    tok = lambda w: pl.BlockSpec((tm, w), lambda i: (i, 0))
    tab = pl.BlockSpec((tm, LANES), lambda i: (i % tiles_per_seq, 0))
    return pl.pallas_call(
        functools.partial(_inproj_kernel, rope=rope),
        grid=(n // tm,),
        in_specs=[tok(D_MODEL),
                  _mod_spec(3, tm, tokens_per_row, row0),
                  _mod_spec(4, tm, tokens_per_row, row0),
                  _const_spec((1, D_MODEL)),
                  _const_spec((D_MODEL, IN_WIDTH_P)),
                  tab, tab],
        out_specs=[tok(w) for w in IN_SPLITS],
        out_shape=[jax.ShapeDtypeStruct((n, w), F32) for w in IN_SPLITS],
        compiler_params=pltpu.CompilerParams(dimension_semantics=("parallel",),
                                             vmem_limit_bytes=VMEM_BIG),
        name="inproj_rope" if rope else "inproj",
    )(x, mod4, mod4, norm_w, w_in_p, cos_t, sin_t)


CONV_TILE = 256
HALO = SUBLANES


def _conv_kernel(xm_ref, xp_ref, xn_ref, db_ref, cw_ref, alog_ref, dtb_ref, qkv_ref, gb_ref, *, tiles):
    t = pl.program_id(1)
    xp = jnp.where(t > 0, xp_ref[...], 0.0)
    xn = jnp.where(t < tiles - 1, xn_ref[...], 0.0)
    xc = jnp.concatenate([xp, xm_ref[...], xn], axis=0)
    rows = xc.shape[0]
    acc = jnp.zeros((CONV_TILE, QKV_A), F32)
    for j in range(CONV_W):
        shift = (CONV_W // 2 - j) % rows
        xs = pltpu.roll(xc, shift, 0) if shift else xc
        acc = acc + xs[HALO:HALO + CONV_TILE] * cw_ref[j:j + 1, :]
    y = _silu(acc)
    for hh in range(2 * H_A):
        sl = slice(hh * DK_A, (hh + 1) * DK_A)
        v = y[:, sl]
        scale = DK_A ** -0.5 if hh < H_A else 1.0
        qkv_ref[:, sl] = v * (lax.rsqrt(jnp.sum(v * v, axis=-1, keepdims=True) + EPS) * scale)
    qkv_ref[:, 2 * H_A * DK_A:] = y[:, 2 * H_A * DK_A:]

    db = db_ref[...]
    xg = db + dtb_ref[...]
    softplus = jnp.maximum(xg, 0.0) + jnp.log1p(jnp.exp(-jnp.abs(xg)))
    g = -jnp.exp(alog_ref[...]) * softplus
    lane = lax.broadcasted_iota(jnp.int32, db.shape, 1)
    gb_ref[...] = jnp.where(lane < DEC_A, g, jnp.where(lane < DEC_A + BETA_A, _sigmoid(db), 0.0))


def _conv(qkv, db, conv_w, alog_p, dtb_p, nb, seq):
    n = qkv.shape[0]
    tiles = seq // CONV_TILE
    halo_per_tile = CONV_TILE // HALO
    last_halo = n // HALO - 1
    main = lambda w: pl.BlockSpec((CONV_TILE, w), lambda b, t: (b * tiles + t, 0))
    prev = pl.BlockSpec((HALO, QKV_A),
                        lambda b, t: (jnp.maximum((b * tiles + t) * halo_per_tile - 1, 0), 0))
    nxt = pl.BlockSpec((HALO, QKV_A),
                       lambda b, t: (jnp.minimum((b * tiles + t + 1) * halo_per_tile, last_halo), 0))
    return pl.pallas_call(
        functools.partial(_conv_kernel, tiles=tiles),
        grid=(nb, tiles),
        in_specs=[main(QKV_A), prev, nxt, main(LANES),
                  pl.BlockSpec((CONV_W, QKV_A), lambda b, t: (0, 0)),
                  pl.BlockSpec((1, LANES), lambda b, t: (0, 0)),
                  pl.BlockSpec((1, LANES), lambda b, t: (0, 0))],
        out_specs=[main(QKV_A), main(LANES)],
        out_shape=[jax.ShapeDtypeStruct((n, QKV_A), F32), jax.ShapeDtypeStruct((n, LANES), F32)],
        compiler_params=pltpu.CompilerParams(dimension_semantics=("parallel", "parallel")),
        name="short_conv",
    )(qkv, qkv, qkv, db, conv_w, alog_p, dtb_p)


TRI_BASE = 16


def _unit_tri_inverse(low):
    n = low.shape[0]
    row = lax.broadcasted_iota(jnp.int32, (n, n), 0)
    col = lax.broadcasted_iota(jnp.int32, (n, n), 1)
    x = -jnp.where(row // TRI_BASE == col // TRI_BASE, low, 0.0)
    t = jnp.where(row == col, 1.0, 0.0) + x
    p = x
    for _ in range(int(np.log2(TRI_BASE)) - 1):
        p = _dot(p, p)
        t = t + _dot(t, p)
    size = TRI_BASE
    while size < n:
        pair = (row // (2 * size) == col // (2 * size)) & (row // size != col // size)
        t = t - _dot(t, _dot(jnp.where(pair, low, 0.0), t))
        size *= 2
    return t


def _scan_kernel(xf_ref, xb_ref, gf_ref, gb_ref, s0f_ref, s0b_ref,
                 of_ref, ob_ref, sf_ref, sb_ref, state, *, steps):
    c = pl.program_id(1)

    @pl.when(c == 0)
    def _():
        state[0:H_A] = s0f_ref[0]
        state[H_A:2 * H_A] = s0b_ref[0]

    row = lax.broadcasted_iota(jnp.int32, (CHUNK, CHUNK), 0)
    col = lax.broadcasted_iota(jnp.int32, (CHUNK, CHUNK), 1)
    for d, (x_ref, g_ref, o_ref) in enumerate(((xf_ref, gf_ref, of_ref), (xb_ref, gb_ref, ob_ref))):
        incl = (row >= col) if d == 0 else (row <= col)
        strict = (row > col) if d == 0 else (row < col)
        gbeta = g_ref[...]
        gc = jnp.dot(jnp.where(incl, 1.0, 0.0), gbeta, precision=lax.Precision.HIGHEST,
                     preferred_element_type=F32)
        gct = gc.T
        for hh in range(H_A):
            r = d * H_A + hh
            q = x_ref[:, hh * DK_A:(hh + 1) * DK_A]
            k = x_ref[:, (H_A + hh) * DK_A:(H_A + hh + 1) * DK_A]
            v = x_ref[:, 2 * H_A * DK_A + hh * DV_A:2 * H_A * DK_A + (hh + 1) * DV_A]
            gcol = gc[:, r:r + 1]
            grow = gct[r:r + 1, :]
            beta = gbeta[:, DEC_A + r:DEC_A + r + 1]
            decay = jnp.exp(jnp.where(incl, gcol - grow, NEG))
            kk = _dot_nt(k, k)
            qk = _dot_nt(q, k)
            tmat = _unit_tri_inverse(jnp.where(strict, kk * beta * decay, 0.0))
            egc = jnp.exp(gcol)
            uw = _dot(tmat, jnp.concatenate([v * beta, k * (beta * egc)], axis=1))
            u = uw[:, :DV_A]
            w = uw[:, DV_A:]
            g_last = gcol[CHUNK - 1:CHUNK] if d == 0 else gcol[0:1]
            s_old = state[r]
            s_bf = s_old.astype(BF)
            v_new = u - _dot(w, s_bf)
            o_ref[:, hh * DV_A:(hh + 1) * DV_A] = _dot(q * egc, s_bf) + _dot(qk * decay, v_new)
            kd = k * jnp.exp(g_last - gcol)
            state[r] = s_old * jnp.exp(g_last) + _dot(kd.T, v_new)

    @pl.when(c == steps - 1)
    def _():
        sf_ref[0] = state[0:H_A]
        sb_ref[0] = state[H_A:2 * H_A]


def _delta_scan(qkvc, gbeta, s0f, s0b, nb, seq):
    n = qkvc.shape[0]
    steps = seq // CHUNK
    fwd = lambda w: pl.BlockSpec((CHUNK, w), lambda b, c: (b * steps + c, 0))
    bwd = lambda w: pl.BlockSpec((CHUNK, w), lambda b, c: (b * steps + steps - 1 - c, 0))
    st = pl.BlockSpec((1, H_A, DK_A, DV_A), lambda b, c: (b, 0, 0, 0))
    st_shape = jax.ShapeDtypeStruct((nb, H_A, DK_A, DV_A), F32)
    return pl.pallas_call(
        functools.partial(_scan_kernel, steps=steps),
        grid=(nb, steps),
        in_specs=[fwd(QKV_A), bwd(QKV_A), fwd(LANES), bwd(LANES), st, st],
        out_specs=[fwd(Z_A), bwd(Z_A), st, st],
        out_shape=[jax.ShapeDtypeStruct((n, Z_A), F32), jax.ShapeDtypeStruct((n, Z_A), F32),
                   st_shape, st_shape],
        scratch_shapes=[pltpu.VMEM((2 * H_A, DK_A, DV_A), F32)],
        compiler_params=pltpu.CompilerParams(dimension_semantics=("parallel", "arbitrary")),
        name="delta_scan",
    )(qkvc, qkvc, gbeta, gbeta, s0f, s0b)


def _attn_core(q, keys, vals, valid, sink_ref, o_ref):
    nq = q.shape[0]
    group = H_B // KV_B
    klane = lax.broadcasted_iota(jnp.int32, keys.shape, 1)
    krot = pltpu.roll(keys, HD_B, 1)
    qlane = lax.broadcasted_iota(jnp.int32, (nq, LANES), 1)
    qlow = qlane < HD_B
    v_bf = vals.astype(BF)
    for g in range(KV_B):
        kdup = jnp.where(klane < HD_B, keys, krot) if g == 0 else jnp.where(klane < HD_B, krot, keys)
        kdup = kdup.astype(BF)
        probs, inv_den = [], []
        for j in range(group):
            head = g * group + j
            qt = q[:, (head // 2) * LANES:(head // 2 + 1) * LANES]
            qh = jnp.where(qlow, qt, 0.0) if head % 2 == 0 else jnp.where(qlow, 0.0, qt)
            s = _dot_nt(qh, kdup)
            if valid is not None:
                s = jnp.where(valid, s, NEG)
            snk = sink_ref[head]
            m = jnp.maximum(jnp.max(s, axis=-1, keepdims=True), snk)
            p = jnp.exp(s - m)
            den = jnp.sum(p, axis=-1, keepdims=True) + jnp.exp(snk - m)
            probs.append(p.astype(BF))
            inv_den.append(1.0 / den)
        o_all = jnp.dot(jnp.concatenate(probs, axis=0), v_bf, preferred_element_type=F32)
        for t in range(group // 2):
            a = o_all[(2 * t) * nq:(2 * t + 1) * nq] * inv_den[2 * t]
            b = o_all[(2 * t + 1) * nq:(2 * t + 2) * nq] * inv_den[2 * t + 1]
            if g == 0:
                tile = jnp.where(qlow, a, pltpu.roll(b, HD_B, 1))
            else:
                tile = jnp.where(qlow, pltpu.roll(a, HD_B, 1), b)
            col = (g * group // 2 + t) * LANES
            o_ref[:, col:col + LANES] = tile


def _ctx_attn_kernel(sink_ref, q_ref, k_ref, v_ref, o_ref):
    _attn_core(q_ref[...] * HD_B ** -0.5, k_ref[...], v_ref[...], None, sink_ref, o_ref)


def _context_attention(q, k, v, sink, nb, seq):
    n = q.shape[0]
    blk = lambda w: pl.BlockSpec((seq, w), lambda b: (b, 0))
    return pl.pallas_call(
        _ctx_attn_kernel,
        grid=(nb,),
        in_specs=[pl.BlockSpec(memory_space=pltpu.SMEM), blk(Q_B), blk(K_B), blk(V_B)],
        out_specs=blk(Q_B),
        out_shape=jax.ShapeDtypeStruct((n, Q_B), F32),
        compiler_params=pltpu.CompilerParams(dimension_semantics=("parallel",)),
        name="context_attention",
    )(sink, q, k, v)


def _win_attn_kernel(sink_ref, q_ref, kp_ref, kc_ref, kn_ref, vp_ref, vc_ref, vn_ref,
                     kx_ref, vx_ref, o_ref, *, blocks):
    i = pl.program_id(1)
    n_ctx = kx_ref.shape[1]
    keys = jnp.concatenate([kp_ref[...], kc_ref[...], kn_ref[...], kx_ref[0]], axis=0)
    vals = jnp.concatenate([vp_ref[...], vc_ref[...], vn_ref[...], vx_ref[0]], axis=0)
    nk = 3 * WINDOW + n_ctx
    r = lax.broadcasted_iota(jnp.int32, (WINDOW, nk), 0)
    j = lax.broadcasted_iota(jnp.int32, (WINDOW, nk), 1)
    local = (j >= r) & (j <= r + 2 * WINDOW)
    local = local & ((j >= WINDOW) | (i > 0)) & ((j < 2 * WINDOW) | (i < blocks - 1))
    valid = local | (j >= 3 * WINDOW)
    _attn_core(q_ref[...] * HD_B ** -0.5, keys, vals, valid, sink_ref, o_ref)


def _window_attention(q, k, v, k_ctx, v_ctx, sink, nb, seq):
    n = q.shape[0]
    blocks = seq // WINDOW
    n_ctx = k_ctx.shape[1]
    cur = lambda w: pl.BlockSpec((WINDOW, w), lambda b, i: (b * blocks + i, 0))
    prev = pl.BlockSpec((WINDOW, K_B), lambda b, i: (b * blocks + jnp.maximum(i - 1, 0), 0))
    nxt = pl.BlockSpec((WINDOW, K_B), lambda b, i: (b * blocks + jnp.minimum(i + 1, blocks - 1), 0))
    ctx = pl.BlockSpec((1, n_ctx, K_B), lambda b, i: (b, 0, 0))
    return pl.pallas_call(
        functools.partial(_win_attn_kernel, blocks=blocks),
        grid=(nb, blocks),
        in_specs=[pl.BlockSpec(memory_space=pltpu.SMEM), cur(Q_B),
                  prev, cur(K_B), nxt, prev, cur(V_B), nxt, ctx, ctx],
        out_specs=cur(Q_B),
        out_shape=jax.ShapeDtypeStruct((n, Q_B), F32),
        compiler_params=pltpu.CompilerParams(dimension_semantics=("parallel", "parallel")),
        name="window_attention",
    )(sink, q, k, k, k, v, v, v, k_ctx, v_ctx)


def _merge_kernel(x_ref, of_ref, ob_ref, z_ref, yb_ref, gate_ref, g2_ref, on_ref,
                  woa_ref, wob_ref, wout_ref, o_ref):
    o = of_ref[...] + ob_ref[...]
    z = z_ref[...]
    heads = []
    for hh in range(H_A):
        sl = slice(hh * DV_A, (hh + 1) * DV_A)
        heads.append(_rms(o[:, sl], on_ref[...]) * _silu(z[:, sl]))
    y_a = jnp.concatenate(heads, axis=1)
    gate = gate_ref[...]
    m = (_sigmoid(gate[:, :D_MODEL]) * _dot(y_a, woa_ref[...])
         + _sigmoid(gate[:, D_MODEL:]) * _dot(yb_ref[...], wob_ref[...]))
    o_ref[...] = x_ref[...] + g2_ref[0, 0] * _dot(m, wout_ref[...])


def _merge(x, o_f, o_b, z, y_b, gates, mod4, tokens_per_row, row0, onorm, w_oa, w_ob, w_out):
    n = x.shape[0]
    tm = 512
    tok = lambda w: pl.BlockSpec((tm, w), lambda i: (i, 0))
    return pl.pallas_call(
        _merge_kernel,
        grid=(n // tm,),
        in_specs=[tok(D_MODEL), tok(Z_A), tok(Z_A), tok(Z_A), tok(Q_B), tok(GATES),
                  _mod_spec(5, tm, tokens_per_row, row0),
                  _const_spec((1, DV_A)),
                  _const_spec((H_A * DV_A, D_MODEL)),
                  _const_spec((H_B * HD_B, D_MODEL)),
                  _const_spec((D_MODEL, D_MODEL))],
        out_specs=tok(D_MODEL),
        out_shape=jax.ShapeDtypeStruct((n, D_MODEL), F32),
        compiler_params=pltpu.CompilerParams(dimension_semantics=("parallel",),
                                             vmem_limit_bytes=VMEM_BIG),
        name="merge",
    )(x, o_f, o_b, z, y_b, gates, mod4, onorm, w_oa, w_ob, w_out)


def _rope_tables(seq):
    rows = seq // GRID_W
    row = jnp.repeat(jnp.arange(rows, dtype=F32), GRID_W)
    col = jnp.tile(jnp.arange(GRID_W, dtype=F32), rows)
    inv = jnp.power(ROPE_BASE, -jnp.arange(ROPE_AXIS_PAIRS, dtype=F32) / ROPE_AXIS_PAIRS)
    ang = jnp.concatenate([row[:, None] * inv, col[:, None] * inv], axis=-1)
    cos, sin = jnp.cos(ang), jnp.sin(ang)
    cos_t = jnp.tile(cos, (1, LANES // (HD_B // 2)))
    sin_t = jnp.tile(jnp.concatenate([-sin, sin], axis=-1), (1, LANES // HD_B))
    return cos_t, sin_t


def _permute_w_in(w_in):
    edges = np.cumsum((0, QKV_A, Z_A, DEC_A, BETA_A, Q_B, K_B, V_B, GATES))
    part = lambda i: w_in[:, int(edges[i]):int(edges[i + 1])]
    pad = jnp.zeros((D_MODEL, LANES - DEC_A - BETA_A), w_in.dtype)
    return jnp.concatenate([part(0), part(1), part(4), part(5), part(6), part(7), part(2), part(3), pad],
                           axis=1).astype(BF)


def kernel(x_prompt, x_sample, state_delta_fwd, state_delta_bwd, cache_k, cache_v, c, c_ctx, ada_w, ada_b, norm_ffn1, ffn1_w13, ffn1_w2, norm_mix, w_in, conv_w, a_log, dt_bias, onorm_a, w_oa, w_ob, w_out, sink, norm_ffn2, ffn2_w13, ffn2_w2, norm_final):
    bp, sp = x_prompt.shape[:2]
    bs, ts = x_sample.shape[:2]
    layer = 0

    mod_rows = 16
    cvec = jnp.concatenate([c_ctx[None, :], c, jnp.zeros((mod_rows - 1 - bs, D_MODEL), F32)], axis=0)
    mod4 = _modulation(cvec, ada_w[layer], ada_b[layer]).reshape(mod_rows, N_MOD, 1, D_MODEL)

    row = lambda a: a.reshape(1, -1)
    lane_pad = lambda a: jnp.pad(a.reshape(1, -1), ((0, 0), (0, LANES - a.size)))
    w13_1, w2_1 = ffn1_w13[layer].astype(BF), ffn1_w2[layer].astype(BF)
    w13_2, w2_2 = ffn2_w13[layer].astype(BF), ffn2_w2[layer].astype(BF)
    w_in_p = _permute_w_in(w_in[layer])
    w_oa_b, w_ob_b, w_out_b = w_oa[layer].astype(BF), w_ob[layer].astype(BF), w_out[layer].astype(BF)
    alog_p, dtb_p = lane_pad(a_log[layer]), lane_pad(dt_bias[layer])
    cos_t, sin_t = _rope_tables(ts)
    final_w = row(norm_final)

    def run(x, nb, seq, tokens_per_row, row0, s0f, s0b, ctx):
        x1 = _ffn(x, mod4, (0, 1, 2), tokens_per_row, row0, row(norm_ffn1[layer]), w13_1, w2_1,
                  final_w, final=False)
        qkv, z, qb, kb, vb, gates, db = _inproj(x1, mod4, tokens_per_row, row0, row(norm_mix[layer]),
                                                w_in_p, cos_t, sin_t, seq, rope=ctx is not None)
        qkvc, gbeta = _conv(qkv, db, conv_w[layer], alog_p, dtb_p, nb, seq)
        o_f, o_b, s_f, s_b = _delta_scan(qkvc, gbeta, s0f, s0b, nb, seq)
        if ctx is None:
            y_b = _context_attention(qb, kb, vb, sink[layer], nb, seq)
        else:
            y_b = _window_attention(qb, kb, vb, ctx[0], ctx[1], sink[layer], nb, seq)
        x2 = _merge(x1, o_f, o_b, z, y_b, gates, mod4, tokens_per_row, row0, row(onorm_a[layer]),
                    w_oa_b, w_ob_b, w_out_b)
        y = _ffn(x2, mod4, (6, 7, 8), tokens_per_row, row0, row(norm_ffn2[layer]), w13_2, w2_2,
                 final_w, final=True)
        return y, s_f, s_b, kb, vb

    zero_state = jnp.zeros((bp, H_A, DK_A, DV_A), F32)
    yp, s_f, s_b, k_p, v_p = run(x_prompt.reshape(bp * sp, D_MODEL), bp, sp, bp * sp, 0,
                                 zero_state, zero_state, None)
    ctx = (cache_k[:, layer].reshape(bs, -1, K_B), cache_v[:, layer].reshape(bs, -1, V_B))
    ys, _, _, _, _ = run(x_sample.reshape(bs * ts, D_MODEL), bs, ts, ts, 1,
                         state_delta_fwd[:, layer], state_delta_bwd[:, layer], ctx)

    return (yp.reshape(bp, sp, D_MODEL), ys.reshape(bs, ts, D_MODEL),
            s_f[:, None], s_b[:, None],
            k_p.reshape(bp, 1, sp, KV_B, HD_B), v_p.reshape(bp, 1, sp, KV_B, HD_B))
```

```python
import functools

import jax
import jax.numpy as jnp
import numpy as np
from jax import lax
from jax.experimental import pallas as pl
from jax.experimental.pallas import tpu as pltpu

D_MODEL = 1024
GRID_W = 64
H_A = 4
DK_A = 128
DV_A = 128
CONV_W = 5
H_B = 8
KV_B = 2
HD_B = 64
WINDOW = 128
ROPE_BASE = 10000.0
ROPE_AXIS_PAIRS = HD_B // 4
D_FF = 2816
HALF_STEP = 0.5
N_MOD = 9
EPS = 1e-6

QKV_A = 2 * H_A * DK_A + H_A * DV_A
Z_A = H_A * DV_A
DEC_A = 2 * H_A
BETA_A = 2 * H_A
Q_B = H_B * HD_B
K_B = KV_B * HD_B
V_B = KV_B * HD_B
GATES = 2 * D_MODEL

LANES = 128
SUBLANES = 8
CHUNK = 128
NEG = -1e30

BF = jnp.bfloat16
F32 = jnp.float32
VMEM_BIG = 56 * 1024 * 1024


def _dot(a, b):
    return jnp.dot(a.astype(BF), b.astype(BF), preferred_element_type=F32)


def _dot_nt(a, b):
    return lax.dot_general(a.astype(BF), b.astype(BF), (((1,), (1,)), ((), ())),
                           preferred_element_type=F32)


def _sigmoid(x):
    return 1.0 / (1.0 + jnp.exp(-x))


def _silu(x):
    return x * _sigmoid(x)


def _rms(x, w):
    return x * lax.rsqrt(jnp.mean(x * x, axis=-1, keepdims=True) + EPS) * w


def _const_spec(shape):
    nd = len(shape)
    return pl.BlockSpec(shape, lambda *_: (0,) * nd, pipeline_mode=pl.Buffered(1))


def _mod_spec(kind, tm, tokens_per_row, row0):
    return pl.BlockSpec((1, 1, 1, D_MODEL),
                        lambda i: (row0 + (i * tm) // tokens_per_row, kind, 0, 0))


def _mod_kernel(c_ref, w_ref, b_ref, o_ref):
    o_ref[...] = _dot(_silu(c_ref[...]), w_ref[...]) + b_ref[...]


def _modulation(cvec, ada_w, ada_b):
    rows = cvec.shape[0]
    n = N_MOD * D_MODEL
    bn = 1152
    return pl.pallas_call(
        _mod_kernel,
        grid=(n // bn,),
        in_specs=[pl.BlockSpec((rows, D_MODEL), lambda j: (0, 0)),
                  pl.BlockSpec((D_MODEL, bn), lambda j: (0, j)),
                  pl.BlockSpec((1, bn), lambda j: (0, j))],
        out_specs=pl.BlockSpec((rows, bn), lambda j: (0, j)),
        out_shape=jax.ShapeDtypeStruct((rows, n), F32),
        name="modulation",
    )(cvec, ada_w, ada_b.reshape(1, n))


def _ffn_kernel(x_ref, sh_ref, sc_ref, g_ref, nw_ref, w13_ref, w2_ref, fw_ref, o_ref, *, ff_chunk, final):
    x = x_ref[...]
    h = (_rms(x, nw_ref[...]) * (1.0 + sc_ref[0, 0]) + sh_ref[0, 0]).astype(BF)
    acc = jnp.zeros(x.shape, F32)
    for j in range(D_FF // ff_chunk):
        a = jnp.dot(h, w13_ref[:, j * ff_chunk:(j + 1) * ff_chunk], preferred_element_type=F32)
        b = jnp.dot(h, w13_ref[:, D_FF + j * ff_chunk:D_FF + (j + 1) * ff_chunk],
                    preferred_element_type=F32)
        t = (_silu(a) * b).astype(BF)
        acc = acc + jnp.dot(t, w2_ref[j * ff_chunk:(j + 1) * ff_chunk, :], preferred_element_type=F32)
    y = x + (HALF_STEP * g_ref[0, 0]) * acc
    if final:
        y = _rms(y, fw_ref[...])
    o_ref[...] = y


def _ffn(x, mod4, kinds, tokens_per_row, row0, norm_w, w13, w2, final_w, final):
    n = x.shape[0]
    tm = 512
    tok = pl.BlockSpec((tm, D_MODEL), lambda i: (i, 0))
    return pl.pallas_call(
        functools.partial(_ffn_kernel, ff_chunk=1408, final=final),
        grid=(n // tm,),
        in_specs=[tok,
                  _mod_spec(kinds[0], tm, tokens_per_row, row0),
                  _mod_spec(kinds[1], tm, tokens_per_row, row0),
                  _mod_spec(kinds[2], tm, tokens_per_row, row0),
                  _const_spec((1, D_MODEL)),
                  _const_spec((D_MODEL, 2 * D_FF)),
                  _const_spec((D_FF, D_MODEL)),
                  _const_spec((1, D_MODEL))],
        out_specs=tok,
        out_shape=jax.ShapeDtypeStruct((n, D_MODEL), F32),
        compiler_params=pltpu.CompilerParams(dimension_semantics=("parallel",),
                                             vmem_limit_bytes=VMEM_BIG),
        name="ffn_final" if final else "ffn",
    )(x, mod4, mod4, mod4, norm_w, w13, w2, final_w)


IN_SPLITS = (QKV_A, Z_A, Q_B, K_B, V_B, GATES, LANES)
IN_WIDTH_P = sum(IN_SPLITS)


def _swap_halves(x):
    w = x.shape[-1]
    lane = lax.broadcasted_iota(jnp.int32, x.shape, 1)
    lower = (lane % HD_B) < (HD_B // 2)
    return jnp.where(lower, pltpu.roll(x, w - HD_B // 2, 1), pltpu.roll(x, HD_B // 2, 1))


def _inproj_kernel(x_ref, sh_ref, sc_ref, nw_ref, w_ref, cos_ref, sin_ref,
                   qkv_ref, z_ref, qb_ref, kb_ref, vb_ref, gate_ref, db_ref, *, rope):
    h = (_rms(x_ref[...], nw_ref[...]) * (1.0 + sc_ref[0, 0]) + sh_ref[0, 0]).astype(BF)
    outs = (qkv_ref, z_ref, qb_ref, kb_ref, vb_ref, gate_ref, db_ref)
    off = 0
    for width, ref in zip(IN_SPLITS, outs):
        p = jnp.dot(h, w_ref[:, off:off + width], preferred_element_type=F32)
        if rope and (ref is qb_ref or ref is kb_ref):
            reps = width // LANES
            c = jnp.concatenate([cos_ref[...]] * reps, axis=1) if reps > 1 else cos_ref[...]
            s = jnp.concatenate([sin_ref[...]] * reps, axis=1) if reps > 1 else sin_ref[...]
            p = p * c + _swap_halves(p) * s
        ref[...] = p
        off += width


def _inproj(x, mod4, tokens_per_row, row0, norm_w, w_in_p, cos_t, sin_t, seq, rope):
    n = x.shape[0]
    tm = 256
    tiles_per_seq = seq // tm
    tok = lambda w: pl.BlockSpec((tm, w), lambda i: (i, 0))
    tab = pl.BlockSpec((tm, LANES), lambda i: (i % tiles_per_seq, 0))
    return pl.pallas_call(
        functools.partial(_inproj_kernel, rope=rope),
        grid=(n // tm,),
        in_specs=[tok(D_MODEL),
                  _mod_spec(3, tm, tokens_per_row, row0),
                  _mod_spec(4, tm, tokens_per_row, row0),
                  _const_spec((1, D_MODEL)),
                  _const_spec((D_MODEL, IN_WIDTH_P)),
                  tab, tab],
        out_specs=[tok(w) for w in IN_SPLITS],
        out_shape=[jax.ShapeDtypeStruct((n, w), F32) for w in IN_SPLITS],
        compiler_params=pltpu.CompilerParams(dimension_semantics=("parallel",),
                                             vmem_limit_bytes=VMEM_BIG),
        name="inproj_rope" if rope else "inproj",
    )(x, mod4, mod4, norm_w, w_in_p, cos_t, sin_t)


CONV_TILE = 256
HALO = SUBLANES


def _conv_kernel(xm_ref, xp_ref, xn_ref, db_ref, cw_ref, alog_ref, dtb_ref, qkv_ref, gb_ref, *, tiles):
    t = pl.program_id(1)
    xp = jnp.where(t > 0, xp_ref[...], 0.0)
    xn = jnp.where(t < tiles - 1, xn_ref[...], 0.0)
    xc = jnp.concatenate([xp, xm_ref[...], xn], axis=0)
    rows = xc.shape[0]
    acc = jnp.zeros((CONV_TILE, QKV_A), F32)
    for j in range(CONV_W):
        shift = (CONV_W // 2 - j) % rows
        xs = pltpu.roll(xc, shift, 0) if shift else xc
        acc = acc + xs[HALO:HALO + CONV_TILE] * cw_ref[j:j + 1, :]
    y = _silu(acc)
    for hh in range(2 * H_A):
        sl = slice(hh * DK_A, (hh + 1) * DK_A)
        v = y[:, sl]
        scale = DK_A ** -0.5 if hh < H_A else 1.0
        qkv_ref[:, sl] = v * (lax.rsqrt(jnp.sum(v * v, axis=-1, keepdims=True) + EPS) * scale)
    qkv_ref[:, 2 * H_A * DK_A:] = y[:, 2 * H_A * DK_A:]

    db = db_ref[...]
    xg = db + dtb_ref[...]
    softplus = jnp.maximum(xg, 0.0) + jnp.log1p(jnp.exp(-jnp.abs(xg)))
    g = -jnp.exp(alog_ref[...]) * softplus
    lane = lax.broadcasted_iota(jnp.int32, db.shape, 1)
    gb_ref[...] = jnp.where(lane < DEC_A, g, jnp.where(lane < DEC_A + BETA_A, _sigmoid(db), 0.0))


def _conv(qkv, db, conv_w, alog_p, dtb_p, nb, seq):
    n = qkv.shape[0]
    tiles = seq // CONV_TILE
    halo_per_tile = CONV_TILE // HALO
    last_halo = n // HALO - 1
    main = lambda w: pl.BlockSpec((CONV_TILE, w), lambda b, t: (b * tiles + t, 0))
    prev = pl.BlockSpec((HALO, QKV_A),
                        lambda b, t: (jnp.maximum((b * tiles + t) * halo_per_tile - 1, 0), 0))
    nxt = pl.BlockSpec((HALO, QKV_A),
                       lambda b, t: (jnp.minimum((b * tiles + t + 1) * halo_per_tile, last_halo), 0))
    return pl.pallas_call(
        functools.partial(_conv_kernel, tiles=tiles),
        grid=(nb, tiles),
        in_specs=[main(QKV_A), prev, nxt, main(LANES),
                  pl.BlockSpec((CONV_W, QKV_A), lambda b, t: (0, 0)),
                  pl.BlockSpec((1, LANES), lambda b, t: (0, 0)),
                  pl.BlockSpec((1, LANES), lambda b, t: (0, 0))],
        out_specs=[main(QKV_A), main(LANES)],
        out_shape=[jax.ShapeDtypeStruct((n, QKV_A), F32), jax.ShapeDtypeStruct((n, LANES), F32)],
        compiler_params=pltpu.CompilerParams(dimension_semantics=("parallel", "parallel")),
        name="short_conv",
    )(qkv, qkv, qkv, db, conv_w, alog_p, dtb_p)


TRI_BASE = 16


def _unit_tri_inverses(lows):
    n = lows[0].shape[0]
    row = lax.broadcasted_iota(jnp.int32, (n, n), 0)
    col = lax.broadcasted_iota(jnp.int32, (n, n), 1)
    base = row // TRI_BASE == col // TRI_BASE
    eye = jnp.where(row == col, 1.0, 0.0)
    ps = [-jnp.where(base, low, 0.0) for low in lows]
    ts = [eye + p for p in ps]
    for _ in range(int(np.log2(TRI_BASE)) - 1):
        ps = [_dot(p, p) for p in ps]
        ts = [t + _dot(t, p) for t, p in zip(ts, ps)]
    size = TRI_BASE
    while size < n:
        pair = (row // (2 * size) == col // (2 * size)) & (row // size != col // size)
        mids = [_dot(jnp.where(pair, low, 0.0), t) for low, t in zip(lows, ts)]
        ts = [t - _dot(t, m) for t, m in zip(ts, mids)]
        size *= 2
    return ts


def _scan_kernel(xf_ref, xb_ref, gf_ref, gb_ref, s0f_ref, s0b_ref,
                 of_ref, ob_ref, sf_ref, sb_ref, state, *, steps):
    c = pl.program_id(1)

    @pl.when(c == 0)
    def _():
        state[0:H_A] = s0f_ref[0]
        state[H_A:2 * H_A] = s0b_ref[0]

    row = lax.broadcasted_iota(jnp.int32, (CHUNK, CHUNK), 0)
    col = lax.broadcasted_iota(jnp.int32, (CHUNK, CHUNK), 1)
    x_refs, o_refs = (xf_ref, xb_ref), (of_ref, ob_ref)
    incl = (row >= col, row <= col)
    strict = (row > col, row < col)
    gbeta = (gf_ref[...], gb_ref[...])
    gc = [jnp.dot(jnp.where(incl[d], 1.0, 0.0), gbeta[d], precision=lax.Precision.HIGHEST,
                  preferred_element_type=F32) for d in range(2)]
    gct = [g.T for g in gc]

    chains = [(d, hh) for d in range(2) for hh in range(H_A)]
    nch = len(chains)
    q = [x_refs[d][:, hh * DK_A:(hh + 1) * DK_A] for d, hh in chains]
    k = [x_refs[d][:, (H_A + hh) * DK_A:(H_A + hh + 1) * DK_A] for d, hh in chains]
    v = [x_refs[d][:, (2 * H_A + hh) * DK_A:(2 * H_A + hh + 1) * DK_A] for d, hh in chains]
    gcol = [gc[d][:, d * H_A + hh:d * H_A + hh + 1] for d, hh in chains]
    beta = [gbeta[d][:, DEC_A + d * H_A + hh:DEC_A + d * H_A + hh + 1] for d, hh in chains]
    decay = [jnp.exp(jnp.where(incl[d], gcol[i] - gct[d][d * H_A + hh:d * H_A + hh + 1, :], NEG))
             for i, (d, hh) in enumerate(chains)]
    kk = [_dot_nt(ki, ki) for ki in k]
    qk = [_dot_nt(qi, ki) for qi, ki in zip(q, k)]
    tmat = _unit_tri_inverses([jnp.where(strict[d], kk[i] * beta[i] * decay[i], 0.0)
                               for i, (d, hh) in enumerate(chains)])
    egc = [jnp.exp(g) for g in gcol]
    uw = [_dot(tmat[i], jnp.concatenate([v[i] * beta[i], k[i] * (beta[i] * egc[i])], axis=1))
          for i in range(nch)]
    g_last = [gcol[i][CHUNK - 1:CHUNK] if d == 0 else gcol[i][0:1] for i, (d, hh) in enumerate(chains)]
    s_old = [state[i] for i in range(nch)]
    s_bf = [s.astype(BF) for s in s_old]
    v_new = [uw[i][:, :DV_A] - _dot(uw[i][:, DV_A:], s_bf[i]) for i in range(nch)]
    for i, (d, hh) in enumerate(chains):
        o_refs[d][:, hh * DV_A:(hh + 1) * DV_A] = (_dot(q[i] * egc[i], s_bf[i])
                                                   + _dot(qk[i] * decay[i], v_new[i]))
    for i in range(nch):
        kd = k[i] * jnp.exp(g_last[i] - gcol[i])
        state[i] = s_old[i] * jnp.exp(g_last[i]) + _dot(kd.T, v_new[i])

    @pl.when(c == steps - 1)
    def _():
        sf_ref[0] = state[0:H_A]
        sb_ref[0] = state[H_A:2 * H_A]


def _delta_scan(qkvc, gbeta, s0f, s0b, nb, seq):
    n = qkvc.shape[0]
    steps = seq // CHUNK
    fwd = lambda w: pl.BlockSpec((CHUNK, w), lambda b, c: (b * steps + c, 0))
    bwd = lambda w: pl.BlockSpec((CHUNK, w), lambda b, c: (b * steps + steps - 1 - c, 0))
    st = pl.BlockSpec((1, H_A, DK_A, DV_A), lambda b, c: (b, 0, 0, 0))
    st_shape = jax.ShapeDtypeStruct((nb, H_A, DK_A, DV_A), F32)
    return pl.pallas_call(
        functools.partial(_scan_kernel, steps=steps),
        grid=(nb, steps),
        in_specs=[fwd(QKV_A), bwd(QKV_A), fwd(LANES), bwd(LANES), st, st],
        out_specs=[fwd(Z_A), bwd(Z_A), st, st],
        out_shape=[jax.ShapeDtypeStruct((n, Z_A), F32), jax.ShapeDtypeStruct((n, Z_A), F32),
                   st_shape, st_shape],
        scratch_shapes=[pltpu.VMEM((2 * H_A, DK_A, DV_A), F32)],
        compiler_params=pltpu.CompilerParams(dimension_semantics=("parallel", "arbitrary")),
        name="delta_scan",
    )(qkvc, qkvc, gbeta, gbeta, s0f, s0b)


def _attn_core(q, keys, vals, valid, sink_ref, o_ref):
    nq = q.shape[0]
    group = H_B // KV_B
    klane = lax.broadcasted_iota(jnp.int32, keys.shape, 1)
    krot = pltpu.roll(keys, HD_B, 1)
    qlane = lax.broadcasted_iota(jnp.int32, (nq, LANES), 1)
    qlow = qlane < HD_B
    v_bf = vals.astype(BF)
    for g in range(KV_B):
        kdup = jnp.where(klane < HD_B, keys, krot) if g == 0 else jnp.where(klane < HD_B, krot, keys)
        kdup = kdup.astype(BF)
        probs, inv_den = [], []
        for j in range(group):
            head = g * group + j
            qt = q[:, (head // 2) * LANES:(head // 2 + 1) * LANES]
            qh = jnp.where(qlow, qt, 0.0) if head % 2 == 0 else jnp.where(qlow, 0.0, qt)
            s = _dot_nt(qh, kdup)
            if valid is not None:
                s = jnp.where(valid, s, NEG)
            snk = sink_ref[head]
            m = jnp.maximum(jnp.max(s, axis=-1, keepdims=True), snk)
            p = jnp.exp(s - m)
            den = jnp.sum(p, axis=-1, keepdims=True) + jnp.exp(snk - m)
            probs.append(p.astype(BF))
            inv_den.append(1.0 / den)
        o_all = jnp.dot(jnp.concatenate(probs, axis=0), v_bf, preferred_element_type=F32)
        for t in range(group // 2):
            a = o_all[(2 * t) * nq:(2 * t + 1) * nq] * inv_den[2 * t]
            b = o_all[(2 * t + 1) * nq:(2 * t + 2) * nq] * inv_den[2 * t + 1]
            if g == 0:
                tile = jnp.where(qlow, a, pltpu.roll(b, HD_B, 1))
            else:
                tile = jnp.where(qlow, pltpu.roll(a, HD_B, 1), b)
            col = (g * group // 2 + t) * LANES
            o_ref[:, col:col + LANES] = tile


def _ctx_attn_kernel(sink_ref, q_ref, k_ref, v_ref, o_ref):
    _attn_core(q_ref[...] * HD_B ** -0.5, k_ref[...], v_ref[...], None, sink_ref, o_ref)


def _context_attention(q, k, v, sink, nb, seq):
    n = q.shape[0]
    blk = lambda w: pl.BlockSpec((seq, w), lambda b: (b, 0))
    return pl.pallas_call(
        _ctx_attn_kernel,
        grid=(nb,),
        in_specs=[pl.BlockSpec(memory_space=pltpu.SMEM), blk(Q_B), blk(K_B), blk(V_B)],
        out_specs=blk(Q_B),
        out_shape=jax.ShapeDtypeStruct((n, Q_B), F32),
        compiler_params=pltpu.CompilerParams(dimension_semantics=("parallel",)),
        name="context_attention",
    )(sink, q, k, v)


def _win_attn_kernel(sink_ref, q_ref, kp_ref, kc_ref, kn_ref, vp_ref, vc_ref, vn_ref,
                     kx_ref, vx_ref, o_ref, *, blocks):
    i = pl.program_id(1)
    n_ctx = kx_ref.shape[1]
    keys = jnp.concatenate([kp_ref[...], kc_ref[...], kn_ref[...], kx_ref[0]], axis=0)
    vals = jnp.concatenate([vp_ref[...], vc_ref[...], vn_ref[...], vx_ref[0]], axis=0)
    nk = 3 * WINDOW + n_ctx
    r = lax.broadcasted_iota(jnp.int32, (WINDOW, nk), 0)
    j = lax.broadcasted_iota(jnp.int32, (WINDOW, nk), 1)
    local = (j >= r) & (j <= r + 2 * WINDOW)
    local = local & ((j >= WINDOW) | (i > 0)) & ((j < 2 * WINDOW) | (i < blocks - 1))
    valid = local | (j >= 3 * WINDOW)
    _attn_core(q_ref[...] * HD_B ** -0.5, keys, vals, valid, sink_ref, o_ref)


def _window_attention(q, k, v, k_ctx, v_ctx, sink, nb, seq):
    n = q.shape[0]
    blocks = seq // WINDOW
    n_ctx = k_ctx.shape[1]
    cur = lambda w: pl.BlockSpec((WINDOW, w), lambda b, i: (b * blocks + i, 0))
    prev = pl.BlockSpec((WINDOW, K_B), lambda b, i: (b * blocks + jnp.maximum(i - 1, 0), 0))
    nxt = pl.BlockSpec((WINDOW, K_B), lambda b, i: (b * blocks + jnp.minimum(i + 1, blocks - 1), 0))
    ctx = pl.BlockSpec((1, n_ctx, K_B), lambda b, i: (b, 0, 0))
    return pl.pallas_call(
        functools.partial(_win_attn_kernel, blocks=blocks),
        grid=(nb, blocks),
        in_specs=[pl.BlockSpec(memory_space=pltpu.SMEM), cur(Q_B),
                  prev, cur(K_B), nxt, prev, cur(V_B), nxt, ctx, ctx],
        out_specs=cur(Q_B),
        out_shape=jax.ShapeDtypeStruct((n, Q_B), F32),
        compiler_params=pltpu.CompilerParams(dimension_semantics=("parallel", "parallel")),
        name="window_attention",
    )(sink, q, k, k, k, v, v, v, k_ctx, v_ctx)


def _merge_kernel(x_ref, of_ref, ob_ref, z_ref, yb_ref, gate_ref, g2_ref, on_ref,
                  woa_ref, wob_ref, wout_ref, o_ref):
    o = of_ref[...] + ob_ref[...]
    z = z_ref[...]
    heads = []
    for hh in range(H_A):
        sl = slice(hh * DV_A, (hh + 1) * DV_A)
        heads.append(_rms(o[:, sl], on_ref[...]) * _silu(z[:, sl]))
    y_a = jnp.concatenate(heads, axis=1)
    gate = gate_ref[...]
    m = (_sigmoid(gate[:, :D_MODEL]) * _dot(y_a, woa_ref[...])
         + _sigmoid(gate[:, D_MODEL:]) * _dot(yb_ref[...], wob_ref[...]))
    o_ref[...] = x_ref[...] + g2_ref[0, 0] * _dot(m, wout_ref[...])


def _merge(x, o_f, o_b, z, y_b, gates, mod4, tokens_per_row, row0, onorm, w_oa, w_ob, w_out):
    n = x.shape[0]
    tm = 512
    tok = lambda w: pl.BlockSpec((tm, w), lambda i: (i, 0))
    return pl.pallas_call(
        _merge_kernel,
        grid=(n // tm,),
        in_specs=[tok(D_MODEL), tok(Z_A), tok(Z_A), tok(Z_A), tok(Q_B), tok(GATES),
                  _mod_spec(5, tm, tokens_per_row, row0),
                  _const_spec((1, DV_A)),
                  _const_spec((H_A * DV_A, D_MODEL)),
                  _const_spec((H_B * HD_B, D_MODEL)),
                  _const_spec((D_MODEL, D_MODEL))],
        out_specs=tok(D_MODEL),
        out_shape=jax.ShapeDtypeStruct((n, D_MODEL), F32),
        compiler_params=pltpu.CompilerParams(dimension_semantics=("parallel",),
                                             vmem_limit_bytes=VMEM_BIG),
        name="merge",
    )(x, o_f, o_b, z, y_b, gates, mod4, onorm, w_oa, w_ob, w_out)


def _rope_tables(seq):
    rows = seq // GRID_W
    row = jnp.repeat(jnp.arange(rows, dtype=F32), GRID_W)
    col = jnp.tile(jnp.arange(GRID_W, dtype=F32), rows)
    inv = jnp.power(ROPE_BASE, -jnp.arange(ROPE_AXIS_PAIRS, dtype=F32) / ROPE_AXIS_PAIRS)
    ang = jnp.concatenate([row[:, None] * inv, col[:, None] * inv], axis=-1)
    cos, sin = jnp.cos(ang), jnp.sin(ang)
    cos_t = jnp.tile(cos, (1, LANES // (HD_B // 2)))
    sin_t = jnp.tile(jnp.concatenate([-sin, sin], axis=-1), (1, LANES // HD_B))
    return cos_t, sin_t


def _permute_w_in(w_in):
    edges = np.cumsum((0, QKV_A, Z_A, DEC_A, BETA_A, Q_B, K_B, V_B, GATES))
    part = lambda i: w_in[:, int(edges[i]):int(edges[i + 1])]
    pad = jnp.zeros((D_MODEL, LANES - DEC_A - BETA_A), w_in.dtype)
    return jnp.concatenate([part(0), part(1), part(4), part(5), part(6), part(7), part(2), part(3), pad],
                           axis=1).astype(BF)


def kernel(x_prompt, x_sample, state_delta_fwd, state_delta_bwd, cache_k, cache_v, c, c_ctx, ada_w, ada_b, norm_ffn1, ffn1_w13, ffn1_w2, norm_mix, w_in, conv_w, a_log, dt_bias, onorm_a, w_oa, w_ob, w_out, sink, norm_ffn2, ffn2_w13, ffn2_w2, norm_final):
    bp, sp = x_prompt.shape[:2]
    bs, ts = x_sample.shape[:2]
    layer = 0

    mod_rows = 16
    cvec = jnp.concatenate([c_ctx[None, :], c, jnp.zeros((mod_rows - 1 - bs, D_MODEL), F32)], axis=0)
    mod4 = _modulation(cvec, ada_w[layer], ada_b[layer]).reshape(mod_rows, N_MOD, 1, D_MODEL)

    row = lambda a: a.reshape(1, -1)
    lane_pad = lambda a: jnp.pad(a.reshape(1, -1), ((0, 0), (0, LANES - a.size)))
    w13_1, w2_1 = ffn1_w13[layer].astype(BF), ffn1_w2[layer].astype(BF)
    w13_2, w2_2 = ffn2_w13[layer].astype(BF), ffn2_w2[layer].astype(BF)
    w_in_p = _permute_w_in(w_in[layer])
    w_oa_b, w_ob_b, w_out_b = w_oa[layer].astype(BF), w_ob[layer].astype(BF), w_out[layer].astype(BF)
    alog_p, dtb_p = lane_pad(a_log[layer]), lane_pad(dt_bias[layer])
    cos_t, sin_t = _rope_tables(ts)
    final_w = row(norm_final)

    def run(x, nb, seq, tokens_per_row, row0, s0f, s0b, ctx):
        x1 = _ffn(x, mod4, (0, 1, 2), tokens_per_row, row0, row(norm_ffn1[layer]), w13_1, w2_1,
                  final_w, final=False)
        qkv, z, qb, kb, vb, gates, db = _inproj(x1, mod4, tokens_per_row, row0, row(norm_mix[layer]),
                                                w_in_p, cos_t, sin_t, seq, rope=ctx is not None)
        qkvc, gbeta = _conv(qkv, db, conv_w[layer], alog_p, dtb_p, nb, seq)
        o_f, o_b, s_f, s_b = _delta_scan(qkvc, gbeta, s0f, s0b, nb, seq)
        if ctx is None:
            y_b = _context_attention(qb, kb, vb, sink[layer], nb, seq)
        else:
            y_b = _window_attention(qb, kb, vb, ctx[0], ctx[1], sink[layer], nb, seq)
        x2 = _merge(x1, o_f, o_b, z, y_b, gates, mod4, tokens_per_row, row0, row(onorm_a[layer]),
                    w_oa_b, w_ob_b, w_out_b)
        y = _ffn(x2, mod4, (6, 7, 8), tokens_per_row, row0, row(norm_ffn2[layer]), w13_2, w2_2,
                 final_w, final=True)
        return y, s_f, s_b, kb, vb

    zero_state = jnp.zeros((bp, H_A, DK_A, DV_A), F32)
    yp, s_f, s_b, k_p, v_p = run(x_prompt.reshape(bp * sp, D_MODEL), bp, sp, bp * sp, 0,
                                 zero_state, zero_state, None)
    ctx = (cache_k[:, layer].reshape(bs, -1, K_B), cache_v[:, layer].reshape(bs, -1, V_B))
    ys, _, _, _, _ = run(x_sample.reshape(bs * ts, D_MODEL), bs, ts, ts, 1,
                         state_delta_fwd[:, layer], state_delta_bwd[:, layer], ctx)

    return (yp.reshape(bp, sp, D_MODEL), ys.reshape(bs, ts, D_MODEL),
            s_f[:, None], s_b[:, None],
            k_p.reshape(bp, 1, sp, KV_B, HD_B), v_p.reshape(bp, 1, sp, KV_B, HD_B))
```

```python
import functools

import jax
import jax.numpy as jnp
import numpy as np
from jax import lax
from jax.experimental import pallas as pl
from jax.experimental.pallas import tpu as pltpu

D_MODEL = 1024
GRID_W = 64
H_A = 4
DK_A = 128
DV_A = 128
CONV_W = 5
H_B = 8
KV_B = 2
HD_B = 64
WINDOW = 128
ROPE_BASE = 10000.0
ROPE_AXIS_PAIRS = HD_B // 4
D_FF = 2816
HALF_STEP = 0.5
N_MOD = 9
EPS = 1e-6

QKV_A = 2 * H_A * DK_A + H_A * DV_A
Z_A = H_A * DV_A
DEC_A = 2 * H_A
BETA_A = 2 * H_A
Q_B = H_B * HD_B
K_B = KV_B * HD_B
V_B = KV_B * HD_B
GATES = 2 * D_MODEL

LANES = 128
SUBLANES = 8
CHUNK = 128
NEG = -1e30

BF = jnp.bfloat16
F32 = jnp.float32
VMEM_BIG = 56 * 1024 * 1024


def _dot(a, b):
    return jnp.dot(a.astype(BF), b.astype(BF), preferred_element_type=F32)


def _dot_nt(a, b):
    return lax.dot_general(a.astype(BF), b.astype(BF), (((1,), (1,)), ((), ())),
                           preferred_element_type=F32)


def _sigmoid(x):
    return 1.0 / (1.0 + jnp.exp(-x))


def _silu(x):
    return x * _sigmoid(x)


def _rms(x, w):
    return x * lax.rsqrt(jnp.mean(x * x, axis=-1, keepdims=True) + EPS) * w


def _const_spec(shape):
    nd = len(shape)
    return pl.BlockSpec(shape, lambda *_: (0,) * nd, pipeline_mode=pl.Buffered(1))


def _mod_spec(kind, tm, tokens_per_row, row0):
    return pl.BlockSpec((1, 1, 1, D_MODEL),
                        lambda i: (row0 + (i * tm) // tokens_per_row, kind, 0, 0))


def _mod_kernel(c_ref, w_ref, b_ref, o_ref):
    o_ref[...] = _dot(_silu(c_ref[...]), w_ref[...]) + b_ref[...]


def _modulation(cvec, ada_w, ada_b):
    rows = cvec.shape[0]
    n = N_MOD * D_MODEL
    bn = 1152
    return pl.pallas_call(
        _mod_kernel,
        grid=(n // bn,),
        in_specs=[pl.BlockSpec((rows, D_MODEL), lambda j: (0, 0)),
                  pl.BlockSpec((D_MODEL, bn), lambda j: (0, j)),
                  pl.BlockSpec((1, bn), lambda j: (0, j))],
        out_specs=pl.BlockSpec((rows, bn), lambda j: (0, j)),
        out_shape=jax.ShapeDtypeStruct((rows, n), F32),
        name="modulation",
    )(cvec, ada_w, ada_b.reshape(1, n))


def _ffn_kernel(x_ref, sh_ref, sc_ref, g_ref, nw_ref, w13_ref, w2_ref, fw_ref, o_ref, *, ff_chunk, final):
    x = x_ref[...]
    h = (_rms(x, nw_ref[...]) * (1.0 + sc_ref[0, 0]) + sh_ref[0, 0]).astype(BF)
    acc = jnp.zeros(x.shape, F32)
    for j in range(D_FF // ff_chunk):
        a = jnp.dot(h, w13_ref[:, j * ff_chunk:(j + 1) * ff_chunk], preferred_element_type=F32)
        b = jnp.dot(h, w13_ref[:, D_FF + j * ff_chunk:D_FF + (j + 1) * ff_chunk],
                    preferred_element_type=F32)
        t = (_silu(a) * b).astype(BF)
        acc = acc + jnp.dot(t, w2_ref[j * ff_chunk:(j + 1) * ff_chunk, :], preferred_element_type=F32)
    y = x + (HALF_STEP * g_ref[0, 0]) * acc
    if final:
        y = _rms(y, fw_ref[...])
    o_ref[...] = y


def _ffn(x, mod4, kinds, tokens_per_row, row0, norm_w, w13, w2, final_w, final):
    n = x.shape[0]
    tm = 512
    tok = pl.BlockSpec((tm, D_MODEL), lambda i: (i, 0))
    return pl.pallas_call(
        functools.partial(_ffn_kernel, ff_chunk=256, final=final),
        grid=(n // tm,),
        in_specs=[tok,
                  _mod_spec(kinds[0], tm, tokens_per_row, row0),
                  _mod_spec(kinds[1], tm, tokens_per_row, row0),
                  _mod_spec(kinds[2], tm, tokens_per_row, row0),
                  _const_spec((1, D_MODEL)),
                  _const_spec((D_MODEL, 2 * D_FF)),
                  _const_spec((D_FF, D_MODEL)),
                  _const_spec((1, D_MODEL))],
        out_specs=tok,
        out_shape=jax.ShapeDtypeStruct((n, D_MODEL), F32),
        compiler_params=pltpu.CompilerParams(dimension_semantics=("parallel",),
                                             vmem_limit_bytes=VMEM_BIG),
        name="ffn_final" if final else "ffn",
    )(x, mod4, mod4, mod4, norm_w, w13, w2, final_w)


IN_WIDTH_P = QKV_A + Z_A + Q_B + K_B + V_B + GATES + LANES


def _swap_halves(x):
    w = x.shape[-1]
    lane = lax.broadcasted_iota(jnp.int32, x.shape, 1)
    lower = (lane % HD_B) < (HD_B // 2)
    return jnp.where(lower, pltpu.roll(x, w - HD_B // 2, 1), pltpu.roll(x, HD_B // 2, 1))


IN_TILE = 256
HALO = SUBLANES
IN_OUT_WIDTHS = (QKV_A, Z_A, Q_B, K_B, V_B, GATES, LANES)


def _rope(p, cos_ref, sin_ref):
    reps = p.shape[1] // LANES
    c = jnp.concatenate([cos_ref[...]] * reps, axis=1) if reps > 1 else cos_ref[...]
    s = jnp.concatenate([sin_ref[...]] * reps, axis=1) if reps > 1 else sin_ref[...]
    return p * c + _swap_halves(p) * s


def _inproj_kernel(xm_ref, xp_ref, xn_ref, sh_ref, sc_ref, nw_ref, w_ref, cos_ref, sin_ref,
                   cw_ref, alog_ref, dtb_ref,
                   qkv_ref, z_ref, qb_ref, kb_ref, vb_ref, gate_ref, gbeta_ref, *, rope, tiles):
    t = pl.program_id(0) % tiles
    x = jnp.concatenate([xp_ref[...], xm_ref[...], xn_ref[...]], axis=0)
    h = (_rms(x, nw_ref[...]) * (1.0 + sc_ref[0, 0]) + sh_ref[0, 0]).astype(BF)
    raw = jnp.dot(h, w_ref[:, :QKV_A], preferred_element_type=F32)
    rows = raw.shape[0]
    rid = lax.broadcasted_iota(jnp.int32, (rows, 1), 0)
    outside = ((rid < HALO) & (t == 0)) | ((rid >= HALO + IN_TILE) & (t == tiles - 1))
    raw = jnp.where(outside, 0.0, raw)

    hm = h[HALO:HALO + IN_TILE]
    off = QKV_A
    z_ref[...] = jnp.dot(hm, w_ref[:, off:off + Z_A], preferred_element_type=F32)
    off += Z_A
    qb = jnp.dot(hm, w_ref[:, off:off + Q_B], preferred_element_type=F32)
    qb_ref[...] = _rope(qb, cos_ref, sin_ref) if rope else qb
    off += Q_B
    kv = jnp.dot(hm, w_ref[:, off:off + K_B + V_B], preferred_element_type=F32)
    kb_ref[...] = _rope(kv[:, :K_B], cos_ref, sin_ref) if rope else kv[:, :K_B]
    vb_ref[...] = kv[:, K_B:]
    off += K_B + V_B
    gate_ref[...] = jnp.dot(hm, w_ref[:, off:off + GATES], preferred_element_type=F32)
    off += GATES
    db = jnp.dot(hm, w_ref[:, off:off + LANES], preferred_element_type=F32)

    xg = db + dtb_ref[...]
    softplus = jnp.maximum(xg, 0.0) + jnp.log1p(jnp.exp(-jnp.abs(xg)))
    g = -jnp.exp(alog_ref[...]) * softplus
    lane = lax.broadcasted_iota(jnp.int32, db.shape, 1)
    gbeta_ref[...] = jnp.where(lane < DEC_A, g, jnp.where(lane < DEC_A + BETA_A, _sigmoid(db), 0.0))

    acc = jnp.zeros((IN_TILE, QKV_A), F32)
    for j in range(CONV_W):
        shift = (CONV_W // 2 - j) % rows
        rs = pltpu.roll(raw, shift, 0) if shift else raw
        acc = acc + rs[HALO:HALO + IN_TILE] * cw_ref[j:j + 1, :]
    y = _silu(acc)
    for hh in range(2 * H_A):
        sl = slice(hh * DK_A, (hh + 1) * DK_A)
        v = y[:, sl]
        scale = DK_A ** -0.5 if hh < H_A else 1.0
        qkv_ref[:, sl] = v * (lax.rsqrt(jnp.sum(v * v, axis=-1, keepdims=True) + EPS) * scale)
    qkv_ref[:, 2 * H_A * DK_A:] = y[:, 2 * H_A * DK_A:]


def _inproj(x, mod4, tokens_per_row, row0, norm_w, w_in_p, cos_t, sin_t, conv_w, alog_p, dtb_p, seq, rope):
    n = x.shape[0]
    tm = IN_TILE
    tiles = seq // tm
    halos_per_tile = tm // HALO
    last_halo = n // HALO - 1
    tok = lambda w: pl.BlockSpec((tm, w), lambda i: (i, 0))
    prev = pl.BlockSpec((HALO, D_MODEL), lambda i: (jnp.maximum(i * halos_per_tile - 1, 0), 0))
    nxt = pl.BlockSpec((HALO, D_MODEL), lambda i: (jnp.minimum((i + 1) * halos_per_tile, last_halo), 0))
    tab = pl.BlockSpec((tm, LANES), lambda i: (i % tiles, 0))
    return pl.pallas_call(
        functools.partial(_inproj_kernel, rope=rope, tiles=tiles),
        grid=(n // tm,),
        in_specs=[tok(D_MODEL), prev, nxt,
                  _mod_spec(3, tm, tokens_per_row, row0),
                  _mod_spec(4, tm, tokens_per_row, row0),
                  _const_spec((1, D_MODEL)),
                  _const_spec((D_MODEL, IN_WIDTH_P)),
                  tab, tab,
                  _const_spec((CONV_W, QKV_A)),
                  _const_spec((1, LANES)),
                  _const_spec((1, LANES))],
        out_specs=[tok(w) for w in IN_OUT_WIDTHS],
        out_shape=[jax.ShapeDtypeStruct((n, w), F32) for w in IN_OUT_WIDTHS],
        compiler_params=pltpu.CompilerParams(dimension_semantics=("parallel",),
                                             vmem_limit_bytes=VMEM_BIG),
        name="inproj_rope" if rope else "inproj",
    )(x, x, x, mod4, mod4, norm_w, w_in_p, cos_t, sin_t, conv_w, alog_p, dtb_p)


TRI_BASE = 16


def _unit_tri_inverses(lows):
    n = lows[0].shape[0]
    row = lax.broadcasted_iota(jnp.int32, (n, n), 0)
    col = lax.broadcasted_iota(jnp.int32, (n, n), 1)
    base = row // TRI_BASE == col // TRI_BASE
    eye = jnp.where(row == col, 1.0, 0.0)
    ps = [-jnp.where(base, low, 0.0) for low in lows]
    ts = [eye + p for p in ps]
    ps = [_dot(p, p) for p in ps]
    for _ in range(int(np.log2(TRI_BASE)) - 2):
        both = [_dot(jnp.concatenate([p.astype(BF), t.astype(BF)], axis=0), p) for p, t in zip(ps, ts)]
        ts = [t + b[n:] for t, b in zip(ts, both)]
        ps = [b[:n] for b in both]
    ts = [t + _dot(t, p) for t, p in zip(ts, ps)]
    size = TRI_BASE
    while size < n:
        pair = (row // (2 * size) == col // (2 * size)) & (row // size != col // size)
        mids = [_dot(jnp.where(pair, low, 0.0), t) for low, t in zip(lows, ts)]
        ts = [t - _dot(t, m) for t, m in zip(ts, mids)]
        size *= 2
    return ts


def _scan_kernel(xf_ref, xb_ref, gf_ref, gb_ref, s0f_ref, s0b_ref,
                 of_ref, ob_ref, sf_ref, sb_ref, state, *, steps):
    c = pl.program_id(1)

    @pl.when(c == 0)
    def _():
        state[0:H_A] = s0f_ref[0]
        state[H_A:2 * H_A] = s0b_ref[0]

    row = lax.broadcasted_iota(jnp.int32, (CHUNK, CHUNK), 0)
    col = lax.broadcasted_iota(jnp.int32, (CHUNK, CHUNK), 1)
    x_refs, o_refs = (xf_ref, xb_ref), (of_ref, ob_ref)
    incl = (row >= col, row <= col)
    strict = (row > col, row < col)
    gbeta = (gf_ref[...], gb_ref[...])
    gc = [jnp.dot(jnp.where(incl[d], 1.0, 0.0), gbeta[d], precision=lax.Precision.HIGHEST,
                  preferred_element_type=F32) for d in range(2)]
    gct = [g.T for g in gc]

    chains = [(d, hh) for d in range(2) for hh in range(H_A)]
    nch = len(chains)
    q = [x_refs[d][:, hh * DK_A:(hh + 1) * DK_A] for d, hh in chains]
    k = [x_refs[d][:, (H_A + hh) * DK_A:(H_A + hh + 1) * DK_A] for d, hh in chains]
    v = [x_refs[d][:, (2 * H_A + hh) * DK_A:(2 * H_A + hh + 1) * DK_A] for d, hh in chains]
    gcol = [gc[d][:, d * H_A + hh:d * H_A + hh + 1] for d, hh in chains]
    beta = [gbeta[d][:, DEC_A + d * H_A + hh:DEC_A + d * H_A + hh + 1] for d, hh in chains]
    decay = [jnp.exp(jnp.where(incl[d], gcol[i] - gct[d][d * H_A + hh:d * H_A + hh + 1, :], NEG))
             for i, (d, hh) in enumerate(chains)]
    kk = [_dot_nt(ki, ki) for ki in k]
    qk = [_dot_nt(qi, ki) for qi, ki in zip(q, k)]
    tmat = _unit_tri_inverses([jnp.where(strict[d], kk[i] * beta[i] * decay[i], 0.0)
                               for i, (d, hh) in enumerate(chains)])
    egc = [jnp.exp(g) for g in gcol]
    uw = [_dot(tmat[i], jnp.concatenate([v[i] * beta[i], k[i] * (beta[i] * egc[i])], axis=1))
          for i in range(nch)]
    g_last = [gcol[i][CHUNK - 1:CHUNK] if d == 0 else gcol[i][0:1] for i, (d, hh) in enumerate(chains)]
    s_old = [state[i] for i in range(nch)]
    s_bf = [s.astype(BF) for s in s_old]
    ws_qs = [_dot(jnp.concatenate([uw[i][:, DV_A:].astype(BF), (q[i] * egc[i]).astype(BF)], axis=0), s_bf[i])
             for i in range(nch)]
    v_new = [uw[i][:, :DV_A] - ws_qs[i][:CHUNK] for i in range(nch)]
    for i, (d, hh) in enumerate(chains):
        o_refs[d][:, hh * DV_A:(hh + 1) * DV_A] = ws_qs[i][CHUNK:] + _dot(qk[i] * decay[i], v_new[i])
    for i in range(nch):
        kd = k[i] * jnp.exp(g_last[i] - gcol[i])
        state[i] = s_old[i] * jnp.exp(g_last[i]) + _dot(kd.T, v_new[i])

    @pl.when(c == steps - 1)
    def _():
        sf_ref[0] = state[0:H_A]
        sb_ref[0] = state[H_A:2 * H_A]


def _delta_scan(qkvc, gbeta, s0f, s0b, nb, seq):
    n = qkvc.shape[0]
    steps = seq // CHUNK
    fwd = lambda w: pl.BlockSpec((CHUNK, w), lambda b, c: (b * steps + c, 0))
    bwd = lambda w: pl.BlockSpec((CHUNK, w), lambda b, c: (b * steps + steps - 1 - c, 0))
    st = pl.BlockSpec((1, H_A, DK_A, DV_A), lambda b, c: (b, 0, 0, 0))
    st_shape = jax.ShapeDtypeStruct((nb, H_A, DK_A, DV_A), F32)
    return pl.pallas_call(
        functools.partial(_scan_kernel, steps=steps),
        grid=(nb, steps),
        in_specs=[fwd(QKV_A), bwd(QKV_A), fwd(LANES), bwd(LANES), st, st],
        out_specs=[fwd(Z_A), bwd(Z_A), st, st],
        out_shape=[jax.ShapeDtypeStruct((n, Z_A), F32), jax.ShapeDtypeStruct((n, Z_A), F32),
                   st_shape, st_shape],
        scratch_shapes=[pltpu.VMEM((2 * H_A, DK_A, DV_A), F32)],
        compiler_params=pltpu.CompilerParams(dimension_semantics=("parallel", "arbitrary")),
        name="delta_scan",
    )(qkvc, qkvc, gbeta, gbeta, s0f, s0b)


def _attn_core(q, keys, vals, valid, sink_ref, write):
    nq = q.shape[0]
    group = H_B // KV_B
    klane = lax.broadcasted_iota(jnp.int32, keys.shape, 1)
    krot = pltpu.roll(keys, HD_B, 1)
    qlow = lax.broadcasted_iota(jnp.int32, (nq, LANES), 1) < HD_B
    v_bf = vals.astype(BF)
    kdup = [jnp.where(klane < HD_B, keys, krot).astype(BF), jnp.where(klane < HD_B, krot, keys).astype(BF)]
    scores = []
    for head in range(H_B):
        qt = q[:, (head // 2) * LANES:(head // 2 + 1) * LANES]
        qh = jnp.where(qlow, qt, 0.0) if head % 2 == 0 else jnp.where(qlow, 0.0, qt)
        scores.append(_dot_nt(qh, kdup[head // group]))
    probs, inv_den = [], []
    for head, s in enumerate(scores):
        if valid is not None:
            s = jnp.where(valid, s, NEG)
        snk = sink_ref[head]
        m = jnp.maximum(jnp.max(s, axis=-1, keepdims=True), snk)
        p = jnp.exp(s - m)
        inv_den.append(1.0 / (jnp.sum(p, axis=-1, keepdims=True) + jnp.exp(snk - m)))
        probs.append(p.astype(BF))
    o_all = [jnp.dot(jnp.concatenate(probs[g * group:(g + 1) * group], axis=0), v_bf,
                     preferred_element_type=F32) for g in range(KV_B)]
    for g in range(KV_B):
        for t in range(group // 2):
            a = o_all[g][(2 * t) * nq:(2 * t + 1) * nq] * inv_den[g * group + 2 * t]
            b = o_all[g][(2 * t + 1) * nq:(2 * t + 2) * nq] * inv_den[g * group + 2 * t + 1]
            if g == 0:
                tile = jnp.where(qlow, a, pltpu.roll(b, HD_B, 1))
            else:
                tile = jnp.where(qlow, pltpu.roll(a, HD_B, 1), b)
            write((g * group // 2 + t) * LANES, tile)


def _ctx_attn_kernel(sink_ref, q_ref, k_ref, v_ref, o_ref):
    def write(col, tile):
        o_ref[:, col:col + LANES] = tile

    _attn_core(q_ref[...] * HD_B ** -0.5, k_ref[...], v_ref[...], None, sink_ref, write)


def _context_attention(q, k, v, sink, nb, seq):
    n = q.shape[0]
    blk = lambda w: pl.BlockSpec((seq, w), lambda b: (b, 0))
    return pl.pallas_call(
        _ctx_attn_kernel,
        grid=(nb,),
        in_specs=[pl.BlockSpec(memory_space=pltpu.SMEM), blk(Q_B), blk(K_B), blk(V_B)],
        out_specs=blk(Q_B),
        out_shape=jax.ShapeDtypeStruct((n, Q_B), F32),
        compiler_params=pltpu.CompilerParams(dimension_semantics=("parallel",)),
        name="context_attention",
    )(sink, q, k, v)


WIN_SUB = 4


def _win_attn_kernel(sink_ref, q_ref, k_ref, v_ref, kx_ref, vx_ref, o_ref):
    step = pl.program_id(1)
    seq = k_ref.shape[0]
    n_local = 3 * WINDOW
    nk = n_local + kx_ref.shape[1]
    r = lax.broadcasted_iota(jnp.int32, (WINDOW, nk), 0)
    j = lax.broadcasted_iota(jnp.int32, (WINDOW, nk), 1)

    def block(sub, carry):
        i = step * WIN_SUB + sub
        start = pl.multiple_of(jnp.clip((i - 1) * WINDOW, 0, seq - n_local), WINDOW)
        keys = jnp.concatenate([k_ref[pl.ds(start, n_local), :], kx_ref[0]], axis=0)
        vals = jnp.concatenate([v_ref[pl.ds(start, n_local), :], vx_ref[0]], axis=0)
        dist = (i * WINDOW - start) + r - j
        valid = ((dist <= WINDOW) & (dist >= -WINDOW)) | (j >= n_local)
        rows = pl.ds(pl.multiple_of(sub * WINDOW, WINDOW), WINDOW)

        def write(col, tile):
            o_ref[rows, col:col + LANES] = tile

        _attn_core(q_ref[rows, :] * HD_B ** -0.5, keys, vals, valid, sink_ref, write)
        return carry

    lax.fori_loop(0, WIN_SUB, block, 0)


def _window_attention(q, k, v, k_ctx, v_ctx, sink, nb, seq):
    n = q.shape[0]
    tq = WIN_SUB * WINDOW
    steps = seq // tq
    n_ctx = k_ctx.shape[1]
    qblk = pl.BlockSpec((tq, Q_B), lambda b, i: (b * steps + i, 0))
    kv = pl.BlockSpec((seq, K_B), lambda b, i: (b, 0))
    ctx = pl.BlockSpec((1, n_ctx, K_B), lambda b, i: (b, 0, 0))
    return pl.pallas_call(
        _win_attn_kernel,
        grid=(nb, steps),
        in_specs=[pl.BlockSpec(memory_space=pltpu.SMEM), qblk, kv, kv, ctx, ctx],
        out_specs=qblk,
        out_shape=jax.ShapeDtypeStruct((n, Q_B), F32),
        compiler_params=pltpu.CompilerParams(dimension_semantics=("parallel", "parallel")),
        name="window_attention",
    )(sink, q, k, v, k_ctx, v_ctx)


def _merge_kernel(x_ref, of_ref, ob_ref, z_ref, yb_ref, gate_ref, g2_ref, on_ref,
                  woa_ref, wob_ref, wout_ref, o_ref):
    o = of_ref[...] + ob_ref[...]
    z = z_ref[...]
    heads = []
    for hh in range(H_A):
        sl = slice(hh * DV_A, (hh + 1) * DV_A)
        heads.append(_rms(o[:, sl], on_ref[...]) * _silu(z[:, sl]))
    y_a = jnp.concatenate(heads, axis=1)
    gate = gate_ref[...]
    m = (_sigmoid(gate[:, :D_MODEL]) * _dot(y_a, woa_ref[...])
         + _sigmoid(gate[:, D_MODEL:]) * _dot(yb_ref[...], wob_ref[...]))
    o_ref[...] = x_ref[...] + g2_ref[0, 0] * _dot(m, wout_ref[...])


def _merge(x, o_f, o_b, z, y_b, gates, mod4, tokens_per_row, row0, onorm, w_oa, w_ob, w_out):
    n = x.shape[0]
    tm = 512
    tok = lambda w: pl.BlockSpec((tm, w), lambda i: (i, 0))
    return pl.pallas_call(
        _merge_kernel,
        grid=(n // tm,),
        in_specs=[tok(D_MODEL), tok(Z_A), tok(Z_A), tok(Z_A), tok(Q_B), tok(GATES),
                  _mod_spec(5, tm, tokens_per_row, row0),
                  _const_spec((1, DV_A)),
                  _const_spec((H_A * DV_A, D_MODEL)),
                  _const_spec((H_B * HD_B, D_MODEL)),
                  _const_spec((D_MODEL, D_MODEL))],
        out_specs=tok(D_MODEL),
        out_shape=jax.ShapeDtypeStruct((n, D_MODEL), F32),
        compiler_params=pltpu.CompilerParams(dimension_semantics=("parallel",),
                                             vmem_limit_bytes=VMEM_BIG),
        name="merge",
    )(x, o_f, o_b, z, y_b, gates, mod4, onorm, w_oa, w_ob, w_out)


def _rope_tables(seq):
    rows = seq // GRID_W
    row = jnp.repeat(jnp.arange(rows, dtype=F32), GRID_W)
    col = jnp.tile(jnp.arange(GRID_W, dtype=F32), rows)
    inv = jnp.power(ROPE_BASE, -jnp.arange(ROPE_AXIS_PAIRS, dtype=F32) / ROPE_AXIS_PAIRS)
    ang = jnp.concatenate([row[:, None] * inv, col[:, None] * inv], axis=-1)
    cos, sin = jnp.cos(ang), jnp.sin(ang)
    cos_t = jnp.tile(cos, (1, LANES // (HD_B // 2)))
    sin_t = jnp.tile(jnp.concatenate([-sin, sin], axis=-1), (1, LANES // HD_B))
    return cos_t, sin_t


def _permute_w_in(w_in):
    edges = np.cumsum((0, QKV_A, Z_A, DEC_A, BETA_A, Q_B, K_B, V_B, GATES))
    part = lambda i: w_in[:, int(edges[i]):int(edges[i + 1])]
    pad = jnp.zeros((D_MODEL, LANES - DEC_A - BETA_A), w_in.dtype)
    return jnp.concatenate([part(0), part(1), part(4), part(5), part(6), part(7), part(2), part(3), pad],
                           axis=1).astype(BF)


def kernel(x_prompt, x_sample, state_delta_fwd, state_delta_bwd, cache_k, cache_v, c, c_ctx, ada_w, ada_b, norm_ffn1, ffn1_w13, ffn1_w2, norm_mix, w_in, conv_w, a_log, dt_bias, onorm_a, w_oa, w_ob, w_out, sink, norm_ffn2, ffn2_w13, ffn2_w2, norm_final):
    bp, sp = x_prompt.shape[:2]
    bs, ts = x_sample.shape[:2]
    layer = 0

    mod_rows = 16
    cvec = jnp.concatenate([c_ctx[None, :], c, jnp.zeros((mod_rows - 1 - bs, D_MODEL), F32)], axis=0)
    mod4 = _modulation(cvec, ada_w[layer], ada_b[layer]).reshape(mod_rows, N_MOD, 1, D_MODEL)

    row = lambda a: a.reshape(1, -1)
    lane_pad = lambda a: jnp.pad(a.reshape(1, -1), ((0, 0), (0, LANES - a.size)))
    w13_1, w2_1 = ffn1_w13[layer].astype(BF), ffn1_w2[layer].astype(BF)
    w13_2, w2_2 = ffn2_w13[layer].astype(BF), ffn2_w2[layer].astype(BF)
    w_in_p = _permute_w_in(w_in[layer])
    w_oa_b, w_ob_b, w_out_b = w_oa[layer].astype(BF), w_ob[layer].astype(BF), w_out[layer].astype(BF)
    alog_p, dtb_p = lane_pad(a_log[layer]), lane_pad(dt_bias[layer])
    cos_t, sin_t = _rope_tables(ts)
    final_w = row(norm_final)

    def run(x, nb, seq, tokens_per_row, row0, s0f, s0b, ctx):
        x1 = _ffn(x, mod4, (0, 1, 2), tokens_per_row, row0, row(norm_ffn1[layer]), w13_1, w2_1,
                  final_w, final=False)
        qkvc, z, qb, kb, vb, gates, gbeta = _inproj(x1, mod4, tokens_per_row, row0, row(norm_mix[layer]),
                                                    w_in_p, cos_t, sin_t, conv_w[layer], alog_p, dtb_p,
                                                    seq, rope=ctx is not None)
        o_f, o_b, s_f, s_b = _delta_scan(qkvc, gbeta, s0f, s0b, nb, seq)
        if ctx is None:
            y_b = _context_attention(qb, kb, vb, sink[layer], nb, seq)
        else:
            y_b = _window_attention(qb, kb, vb, ctx[0], ctx[1], sink[layer], nb, seq)
        x2 = _merge(x1, o_f, o_b, z, y_b, gates, mod4, tokens_per_row, row0, row(onorm_a[layer]),
                    w_oa_b, w_ob_b, w_out_b)
        y = _ffn(x2, mod4, (6, 7, 8), tokens_per_row, row0, row(norm_ffn2[layer]), w13_2, w2_2,
                 final_w, final=True)
        return y, s_f, s_b, kb, vb

    zero_state = jnp.zeros((bp, H_A, DK_A, DV_A), F32)
    yp, s_f, s_b, k_p, v_p = run(x_prompt.reshape(bp * sp, D_MODEL), bp, sp, bp * sp, 0,
                                 zero_state, zero_state, None)
    ctx = (cache_k[:, layer].reshape(bs, -1, K_B), cache_v[:, layer].reshape(bs, -1, V_B))
    ys, _, _, _, _ = run(x_sample.reshape(bs * ts, D_MODEL), bs, ts, ts, 1,
                         state_delta_fwd[:, layer], state_delta_bwd[:, layer], ctx)

    return (yp.reshape(bp, sp, D_MODEL), ys.reshape(bs, ts, D_MODEL),
            s_f[:, None], s_b[:, None],
            k_p.reshape(bp, 1, sp, KV_B, HD_B), v_p.reshape(bp, 1, sp, KV_B, HD_B))
```

```python
import functools

import jax
import jax.numpy as jnp
import numpy as np
from jax import lax
from jax.experimental import pallas as pl
from jax.experimental.pallas import tpu as pltpu

D_MODEL = 1024
GRID_W = 64
H_A = 4
DK_A = 128
DV_A = 128
CONV_W = 5
H_B = 8
KV_B = 2
HD_B = 64
WINDOW = 128
ROPE_BASE = 10000.0
ROPE_AXIS_PAIRS = HD_B // 4
D_FF = 2816
HALF_STEP = 0.5
N_MOD = 9
EPS = 1e-6

QKV_A = 2 * H_A * DK_A + H_A * DV_A
Z_A = H_A * DV_A
DEC_A = 2 * H_A
BETA_A = 2 * H_A
Q_B = H_B * HD_B
K_B = KV_B * HD_B
V_B = KV_B * HD_B
GATES = 2 * D_MODEL

LANES = 128
SUBLANES = 8
CHUNK = 128
NEG = -1e30

BF = jnp.bfloat16
F32 = jnp.float32
VMEM_BIG = 56 * 1024 * 1024


def _dot(a, b):
    return jnp.dot(a.astype(BF), b.astype(BF), preferred_element_type=F32)


def _dot_nt(a, b):
    return lax.dot_general(a.astype(BF), b.astype(BF), (((1,), (1,)), ((), ())),
                           preferred_element_type=F32)


def _sigmoid(x):
    return 1.0 / (1.0 + jnp.exp(-x))


def _silu(x):
    return x * _sigmoid(x)


def _rms(x, w):
    return x * lax.rsqrt(jnp.mean(x * x, axis=-1, keepdims=True) + EPS) * w


def _modulated_norm(x, nw_ref, sh_ref, sc_ref):
    return (_rms(x, nw_ref[...]) * (1.0 + sc_ref[0, 0]) + sh_ref[0, 0]).astype(BF)


def _const_spec(shape):
    nd = len(shape)
    return pl.BlockSpec(shape, lambda *_: (0,) * nd, pipeline_mode=pl.Buffered(1))


def _mod_spec(kind, tm, tokens_per_row, row0):
    return pl.BlockSpec((1, 1, 1, D_MODEL),
                        lambda i: (row0 + (i * tm) // tokens_per_row, kind, 0, 0))


def _mod_kernel(c_ref, w_ref, b_ref, o_ref):
    o_ref[...] = _dot(_silu(c_ref[...]), w_ref[...]) + b_ref[...]


def _modulation(cvec, ada_w, ada_b):
    rows = cvec.shape[0]
    n = N_MOD * D_MODEL
    bn = 1152
    return pl.pallas_call(
        _mod_kernel,
        grid=(n // bn,),
        in_specs=[pl.BlockSpec((rows, D_MODEL), lambda j: (0, 0)),
                  pl.BlockSpec((D_MODEL, bn), lambda j: (0, j)),
                  pl.BlockSpec((1, bn), lambda j: (0, j))],
        out_specs=pl.BlockSpec((rows, bn), lambda j: (0, j)),
        out_shape=jax.ShapeDtypeStruct((rows, n), F32),
        name="modulation",
    )(cvec, ada_w, ada_b.reshape(1, n))


def _ffn_kernel(x_ref, sh_ref, sc_ref, g_ref, nw_ref, w13_ref, w2_ref, fw_ref, o_ref, *, ff_chunk, final):
    x = x_ref[...]
    h = _modulated_norm(x, nw_ref, sh_ref, sc_ref)
    acc = jnp.zeros(x.shape, F32)
    for j in range(D_FF // ff_chunk):
        a = jnp.dot(h, w13_ref[:, j * ff_chunk:(j + 1) * ff_chunk], preferred_element_type=F32)
        b = jnp.dot(h, w13_ref[:, D_FF + j * ff_chunk:D_FF + (j + 1) * ff_chunk],
                    preferred_element_type=F32)
        t = (_silu(a) * b).astype(BF)
        acc = acc + jnp.dot(t, w2_ref[j * ff_chunk:(j + 1) * ff_chunk, :], preferred_element_type=F32)
    y = x + (HALF_STEP * g_ref[0, 0]) * acc
    if final:
        y = _rms(y, fw_ref[...])
    o_ref[...] = y


def _ffn(x, mod4, kinds, tokens_per_row, row0, norm_w, w13, w2, final_w, final):
    n = x.shape[0]
    tm = 512
    tok = pl.BlockSpec((tm, D_MODEL), lambda i: (i, 0))
    return pl.pallas_call(
        functools.partial(_ffn_kernel, ff_chunk=256, final=final),
        grid=(n // tm,),
        in_specs=[tok,
                  _mod_spec(kinds[0], tm, tokens_per_row, row0),
                  _mod_spec(kinds[1], tm, tokens_per_row, row0),
                  _mod_spec(kinds[2], tm, tokens_per_row, row0),
                  _const_spec((1, D_MODEL)),
                  _const_spec((D_MODEL, 2 * D_FF)),
                  _const_spec((D_FF, D_MODEL)),
                  _const_spec((1, D_MODEL))],
        out_specs=tok,
        out_shape=jax.ShapeDtypeStruct((n, D_MODEL), F32),
        compiler_params=pltpu.CompilerParams(dimension_semantics=("parallel",),
                                             vmem_limit_bytes=VMEM_BIG),
        name="ffn_final" if final else "ffn",
    )(x, mod4, mod4, mod4, norm_w, w13, w2, final_w)


IN_WIDTH_P = QKV_A + Q_B + K_B + V_B + LANES
IN_TILE = 256
HALO = SUBLANES


def _swap_halves(x):
    w = x.shape[-1]
    lane = lax.broadcasted_iota(jnp.int32, x.shape, 1)
    lower = (lane % HD_B) < (HD_B // 2)
    return jnp.where(lower, pltpu.roll(x, w - HD_B // 2, 1), pltpu.roll(x, HD_B // 2, 1))


def _rope(p, cos_ref, sin_ref):
    reps = p.shape[1] // LANES
    c = jnp.concatenate([cos_ref[...]] * reps, axis=1) if reps > 1 else cos_ref[...]
    s = jnp.concatenate([sin_ref[...]] * reps, axis=1) if reps > 1 else sin_ref[...]
    return p * c + _swap_halves(p) * s


def _inproj_kernel(xm_ref, xp_ref, xn_ref, sh_ref, sc_ref, nw_ref, w_ref, cos_ref, sin_ref,
                   cw_ref, alog_ref, dtb_ref, qkv_ref, qb_ref, kb_ref, vb_ref, gbeta_ref, *rest,
                   latent, tiles):
    t = pl.program_id(0) % tiles
    x = jnp.concatenate([xp_ref[...], xm_ref[...], xn_ref[...]], axis=0)
    h = _modulated_norm(x, nw_ref, sh_ref, sc_ref)
    rows = x.shape[0]
    blocks = rows // SUBLANES
    rid = lax.broadcasted_iota(jnp.int32, (rows, 1), 0)
    outside = ((rid < HALO) & (t == 0)) | ((rid >= HALO + IN_TILE) & (t == tiles - 1))
    hm = h[HALO:HALO + IN_TILE]

    def project(first, width, lhs=hm):
        return jnp.dot(lhs, w_ref[:, first:first + width], preferred_element_type=F32)

    def conv_section(raw, first):
        width = raw.shape[1]
        r3 = jnp.where(outside, 0.0, raw).reshape(blocks, SUBLANES, width)
        sub = lax.broadcasted_iota(jnp.int32, (1, SUBLANES, width), 1)
        acc = None
        for j in range(CONV_W):
            s = CONV_W // 2 - j
            tap = cw_ref[j:j + 1, first:first + width].reshape(1, 1, width)
            if s == 0:
                sh = r3[1:blocks - 1]
            else:
                rot = pltpu.roll(r3, s % SUBLANES, 1)
                if s > 0:
                    sh = jnp.where(sub >= s, rot[1:blocks - 1], rot[0:blocks - 2])
                else:
                    sh = jnp.where(sub < SUBLANES + s, rot[1:blocks - 1], rot[2:blocks])
            acc = sh * tap if acc is None else acc + sh * tap
        y = _silu(acc.reshape(IN_TILE, width))
        for hh in range(width // DK_A):
            v = y[:, hh * DK_A:(hh + 1) * DK_A]
            if first < 2 * H_A * DK_A:
                scale = DK_A ** -0.5 if first < H_A * DK_A else 1.0
                v = v * (lax.rsqrt(jnp.sum(v * v, axis=-1, keepdims=True) + EPS) * scale)
            qkv_ref[:, first + hh * DK_A:first + (hh + 1) * DK_A] = v

    sec = H_A * DK_A
    qb0 = QKV_A
    kv0 = qb0 + Q_B
    db0 = kv0 + K_B + V_B
    for i in range(QKV_A // sec):
        conv_section(project(i * sec, sec, h), i * sec)
    qb = project(qb0, Q_B)
    qb_ref[...] = _rope(qb, cos_ref, sin_ref) if latent else qb
    kv = project(kv0, K_B + V_B)
    kb_ref[...] = _rope(kv[:, :K_B], cos_ref, sin_ref) if latent else kv[:, :K_B]
    vb_ref[...] = kv[:, K_B:]
    if not latent:
        kt_ref, vt_ref = rest
        kt_ref[0] = kv[:, :K_B].T
        vt_ref[0] = kv[:, K_B:].T
    db = project(db0, LANES)

    xg = db + dtb_ref[...]
    softplus = jnp.maximum(xg, 0.0) + jnp.log1p(jnp.exp(-jnp.abs(xg)))
    g = -jnp.exp(alog_ref[...]) * softplus
    lane = lax.broadcasted_iota(jnp.int32, db.shape, 1)
    gbeta_ref[...] = jnp.where(lane < DEC_A, g, jnp.where(lane < DEC_A + BETA_A, _sigmoid(db), 0.0))


def _inproj(x, mod4, tokens_per_row, row0, norm_w, w_mix, cos_t, sin_t, conv_w, alog_p, dtb_p, seq, latent):
    n = x.shape[0]
    tm = IN_TILE
    tiles = seq // tm
    halos_per_tile = tm // HALO
    last_halo = n // HALO - 1
    tok = lambda w: pl.BlockSpec((tm, w), lambda i: (i, 0))
    prev = pl.BlockSpec((HALO, D_MODEL), lambda i: (jnp.maximum(i * halos_per_tile - 1, 0), 0))
    nxt = pl.BlockSpec((HALO, D_MODEL), lambda i: (jnp.minimum((i + 1) * halos_per_tile, last_halo), 0))
    tab = pl.BlockSpec((tm, LANES), lambda i: (i % tiles, 0))
    widths = (QKV_A, Q_B, K_B, V_B, LANES)
    out_specs = [tok(w) for w in widths]
    out_shape = [jax.ShapeDtypeStruct((n, w), F32) for w in widths]
    if not latent:
        assert tiles == 1
        out_specs += [pl.BlockSpec((1, K_B, tm), lambda i: (i, 0, 0))] * 2
        out_shape += [jax.ShapeDtypeStruct((n // tm, K_B, tm), F32)] * 2
    return pl.pallas_call(
        functools.partial(_inproj_kernel, latent=latent, tiles=tiles),
        grid=(n // tm,),
        in_specs=[tok(D_MODEL), prev, nxt,
                  _mod_spec(3, tm, tokens_per_row, row0),
                  _mod_spec(4, tm, tokens_per_row, row0),
                  _const_spec((1, D_MODEL)),
                  _const_spec((D_MODEL, IN_WIDTH_P)),
                  tab, tab,
                  _const_spec((CONV_W, QKV_A)),
                  _const_spec((1, LANES)),
                  _const_spec((1, LANES))],
        out_specs=out_specs,
        out_shape=out_shape,
        compiler_params=pltpu.CompilerParams(dimension_semantics=("parallel",),
                                             vmem_limit_bytes=VMEM_BIG),
        name="inproj_latent" if latent else "inproj_context",
    )(x, x, x, mod4, mod4, norm_w, w_mix, cos_t, sin_t, conv_w, alog_p, dtb_p)


TRI_BASE = 16
SCAN_CHUNKS_PER_STEP = 4


def _unit_tri_inverses(lows):
    n = lows[0].shape[0]
    row = lax.broadcasted_iota(jnp.int32, (n, n), 0)
    col = lax.broadcasted_iota(jnp.int32, (n, n), 1)
    base = row // TRI_BASE == col // TRI_BASE
    eye = jnp.where(row == col, 1.0, 0.0)
    ps = [-jnp.where(base, low, 0.0) for low in lows]
    ts = [eye + p for p in ps]
    ps = [_dot(p, p) for p in ps]
    for _ in range(int(np.log2(TRI_BASE)) - 2):
        both = [_dot(jnp.concatenate([p.astype(BF), t.astype(BF)], axis=0), p) for p, t in zip(ps, ts)]
        ts = [t + b[n:] for t, b in zip(ts, both)]
        ps = [b[:n] for b in both]
    ts = [t + _dot(t, p) for t, p in zip(ts, ps)]
    size = TRI_BASE
    while size < n:
        pair = (row // (2 * size) == col // (2 * size)) & (row // size != col // size)
        mids = [_dot(jnp.where(pair, low, 0.0), t) for low, t in zip(lows, ts)]
        ts = [t - _dot(t, m) for t, m in zip(ts, mids)]
        size *= 2
    return ts


def _scan_kernel(*refs, steps, chunks, has_init, emit_state):
    xf_ref, xb_ref, gf_ref, gb_ref = refs[:4]
    rest = list(refs[4:])
    s0_refs = (rest.pop(0), rest.pop(0)) if has_init else None
    of_ref, ob_ref = rest.pop(0), rest.pop(0)
    out_state_refs = (rest.pop(0), rest.pop(0)) if emit_state else None
    state, = rest
    c = pl.program_id(1)

    @pl.when(c == 0)
    def _():
        if has_init:
            state[0:H_A] = s0_refs[0][0]
            state[H_A:2 * H_A] = s0_refs[1][0]
        else:
            state[...] = jnp.zeros(state.shape, F32)

    row = lax.broadcasted_iota(jnp.int32, (CHUNK, CHUNK), 0)
    col = lax.broadcasted_iota(jnp.int32, (CHUNK, CHUNK), 1)
    x_refs, g_refs, o_refs = (xf_ref, xb_ref), (gf_ref, gb_ref), (of_ref, ob_ref)
    incl = (row >= col, row <= col)
    strict = (row > col, row < col)
    chains = [(d, hh) for d in range(2) for hh in range(H_A)]
    nch = len(chains)

    def one_chunk(sub, carry):
        rows = (pl.ds(pl.multiple_of(sub * CHUNK, CHUNK), CHUNK),
                pl.ds(pl.multiple_of((chunks - 1 - sub) * CHUNK, CHUNK), CHUNK))
        gbeta = [g_refs[d][rows[d], :] for d in range(2)]
        gc = [jnp.dot(jnp.where(incl[d], 1.0, 0.0), gbeta[d], precision=lax.Precision.HIGHEST,
                      preferred_element_type=F32) for d in range(2)]
        gct = [g.T for g in gc]
        q = [x_refs[d][rows[d], hh * DK_A:(hh + 1) * DK_A] for d, hh in chains]
        k = [x_refs[d][rows[d], (H_A + hh) * DK_A:(H_A + hh + 1) * DK_A] for d, hh in chains]
        v = [x_refs[d][rows[d], (2 * H_A + hh) * DK_A:(2 * H_A + hh + 1) * DK_A] for d, hh in chains]
        gcol = [gc[d][:, d * H_A + hh:d * H_A + hh + 1] for d, hh in chains]
        beta = [gbeta[d][:, DEC_A + d * H_A + hh:DEC_A + d * H_A + hh + 1] for d, hh in chains]
        decay = [jnp.exp(jnp.where(incl[d], gcol[i] - gct[d][d * H_A + hh:d * H_A + hh + 1, :], NEG))
                 for i, (d, hh) in enumerate(chains)]
        qk_kk = [_dot_nt(jnp.concatenate([q[i].astype(BF), k[i].astype(BF)], axis=0), k[i]) for i in range(nch)]
        tmat = _unit_tri_inverses([jnp.where(strict[d], qk_kk[i][CHUNK:] * beta[i] * decay[i], 0.0)
                                   for i, (d, hh) in enumerate(chains)])
        egc = [jnp.exp(g) for g in gcol]
        uw = [_dot(tmat[i], jnp.concatenate([v[i] * beta[i], k[i] * (beta[i] * egc[i])], axis=1))
              for i in range(nch)]
        g_last = [gcol[i][CHUNK - 1:CHUNK] if d == 0 else gcol[i][0:1] for i, (d, hh) in enumerate(chains)]
        s_old = [state[i] for i in range(nch)]
        s_bf = [s.astype(BF) for s in s_old]
        ws_qs = [_dot(jnp.concatenate([uw[i][:, DV_A:].astype(BF), (q[i] * egc[i]).astype(BF)], axis=0), s_bf[i])
                 for i in range(nch)]
        v_new = [uw[i][:, :DV_A] - ws_qs[i][:CHUNK] for i in range(nch)]
        for i, (d, hh) in enumerate(chains):
            o_refs[d][rows[d], hh * DV_A:(hh + 1) * DV_A] = (ws_qs[i][CHUNK:]
                                                            + _dot(qk_kk[i][:CHUNK] * decay[i], v_new[i]))
        for i in range(nch):
            kd = k[i] * jnp.exp(g_last[i] - gcol[i])
            state[i] = s_old[i] * jnp.exp(g_last[i]) + _dot(kd.T, v_new[i])
        return carry

    lax.fori_loop(0, chunks, one_chunk, 0)

    if emit_state:
        @pl.when(c == steps - 1)
        def _():
            out_state_refs[0][0] = state[0:H_A]
            out_state_refs[1][0] = state[H_A:2 * H_A]


def _delta_scan(qkvc, gbeta, init, nb, seq, emit_state):
    n = qkvc.shape[0]
    chunks = min(SCAN_CHUNKS_PER_STEP, seq // CHUNK)
    rows = chunks * CHUNK
    steps = seq // rows
    fwd = lambda w: pl.BlockSpec((rows, w), lambda b, c: (b * steps + c, 0))
    bwd = lambda w: pl.BlockSpec((rows, w), lambda b, c: (b * steps + steps - 1 - c, 0))
    st = pl.BlockSpec((1, H_A, DK_A, DV_A), lambda b, c: (b, 0, 0, 0))
    st_shape = jax.ShapeDtypeStruct((nb, H_A, DK_A, DV_A), F32)
    o_shape = jax.ShapeDtypeStruct((n, Z_A), F32)
    has_init = init is not None
    return pl.pallas_call(
        functools.partial(_scan_kernel, steps=steps, chunks=chunks, has_init=has_init, emit_state=emit_state),
        grid=(nb, steps),
        in_specs=[fwd(QKV_A), bwd(QKV_A), fwd(LANES), bwd(LANES)] + ([st, st] if has_init else []),
        out_specs=[fwd(Z_A), bwd(Z_A)] + ([st, st] if emit_state else []),
        out_shape=[o_shape, o_shape] + ([st_shape, st_shape] if emit_state else []),
        scratch_shapes=[pltpu.VMEM((2 * H_A, DK_A, DV_A), F32)],
        compiler_params=pltpu.CompilerParams(dimension_semantics=("parallel", "arbitrary")),
        name="delta_scan",
    )(qkvc, qkvc, gbeta, gbeta, *(init if has_init else ()))


def _attn_core(q, keys, vals, valid, sink_ref, write):
    nq = q.shape[0]
    group = H_B // KV_B
    klane = lax.broadcasted_iota(jnp.int32, keys.shape, 1)
    krot = pltpu.roll(keys, HD_B, 1)
    qlow = lax.broadcasted_iota(jnp.int32, (nq, LANES), 1) < HD_B
    v_bf = vals.astype(BF)
    kdup = [jnp.where(klane < HD_B, keys, krot).astype(BF), jnp.where(klane < HD_B, krot, keys).astype(BF)]
    scores = []
    for head in range(H_B):
        qt = q[:, (head // 2) * LANES:(head // 2 + 1) * LANES]
        qh = jnp.where(qlow, qt, 0.0) if head % 2 == 0 else jnp.where(qlow, 0.0, qt)
        scores.append(_dot_nt(qh, kdup[head // group]))
    probs, inv_den = [], []
    for head, s in enumerate(scores):
        if valid is not None:
            s = jnp.where(valid, s, NEG)
        snk = sink_ref[head]
        m = jnp.maximum(jnp.max(s, axis=-1, keepdims=True), snk)
        p = jnp.exp(s - m)
        inv_den.append(1.0 / (jnp.sum(p, axis=-1, keepdims=True) + jnp.exp(snk - m)))
        probs.append(p.astype(BF))
    o_all = [jnp.dot(jnp.concatenate(probs[g * group:(g + 1) * group], axis=0), v_bf,
                     preferred_element_type=F32) for g in range(KV_B)]
    for g in range(KV_B):
        for t in range(group // 2):
            a = o_all[g][(2 * t) * nq:(2 * t + 1) * nq] * inv_den[g * group + 2 * t]
            b = o_all[g][(2 * t + 1) * nq:(2 * t + 2) * nq] * inv_den[g * group + 2 * t + 1]
            if g == 0:
                tile = jnp.where(qlow, a, pltpu.roll(b, HD_B, 1))
            else:
                tile = jnp.where(qlow, pltpu.roll(a, HD_B, 1), b)
            write((g * group // 2 + t) * LANES, tile)


def _ctx_attn_kernel(sink_ref, q_ref, k_ref, v_ref, o_ref):
    def write(col, tile):
        o_ref[:, col:col + LANES] = tile

    _attn_core(q_ref[...] * HD_B ** -0.5, k_ref[...], v_ref[...], None, sink_ref, write)


def _context_attention(q, k, v, sink, nb, seq):
    n = q.shape[0]
    blk = lambda w: pl.BlockSpec((seq, w), lambda b: (b, 0))
    return pl.pallas_call(
        _ctx_attn_kernel,
        grid=(nb,),
        in_specs=[pl.BlockSpec(memory_space=pltpu.SMEM), blk(Q_B), blk(K_B), blk(V_B)],
        out_specs=blk(Q_B),
        out_shape=jax.ShapeDtypeStruct((n, Q_B), F32),
        compiler_params=pltpu.CompilerParams(dimension_semantics=("parallel",)),
        name="context_attention",
    )(sink, q, k, v)


WIN_SUB = 4


def _win_attn_kernel(sink_ref, q_ref, k_ref, v_ref, kx_ref, vx_ref, o_ref):
    step = pl.program_id(1)
    seq = k_ref.shape[0]
    n_local = 3 * WINDOW
    nk = n_local + kx_ref.shape[1]
    r = lax.broadcasted_iota(jnp.int32, (WINDOW, nk), 0)
    j = lax.broadcasted_iota(jnp.int32, (WINDOW, nk), 1)

    def block(sub, carry):
        i = step * WIN_SUB + sub
        start = pl.multiple_of(jnp.clip((i - 1) * WINDOW, 0, seq - n_local), WINDOW)
        keys = jnp.concatenate([k_ref[pl.ds(start, n_local), :], kx_ref[0]], axis=0)
        vals = jnp.concatenate([v_ref[pl.ds(start, n_local), :], vx_ref[0]], axis=0)
        dist = (i * WINDOW - start) + r - j
        valid = ((dist <= WINDOW) & (dist >= -WINDOW)) | (j >= n_local)
        rows = pl.ds(pl.multiple_of(sub * WINDOW, WINDOW), WINDOW)

        def write(col, tile):
            o_ref[rows, col:col + LANES] = tile

        _attn_core(q_ref[rows, :] * HD_B ** -0.5, keys, vals, valid, sink_ref, write)
        return carry

    lax.fori_loop(0, WIN_SUB, block, 0)


def _window_attention(q, k, v, k_ctx, v_ctx, sink, nb, seq):
    n = q.shape[0]
    tq = WIN_SUB * WINDOW
    steps = seq // tq
    n_ctx = k_ctx.shape[1]
    qblk = pl.BlockSpec((tq, Q_B), lambda b, i: (b * steps + i, 0))
    kv = pl.BlockSpec((seq, K_B), lambda b, i: (b, 0))
    ctx = pl.BlockSpec((1, n_ctx, K_B), lambda b, i: (b, 0, 0))
    return pl.pallas_call(
        _win_attn_kernel,
        grid=(nb, steps),
        in_specs=[pl.BlockSpec(memory_space=pltpu.SMEM), qblk, kv, kv, ctx, ctx],
        out_specs=qblk,
        out_shape=jax.ShapeDtypeStruct((n, Q_B), F32),
        compiler_params=pltpu.CompilerParams(dimension_semantics=("parallel", "parallel")),
        name="window_attention",
    )(sink, q, k, v, k_ctx, v_ctx)


def _merge_kernel(x_ref, of_ref, ob_ref, yb_ref, sh_ref, sc_ref, g2_ref, nw_ref, on_ref,
                  wzg_ref, woa_ref, wob_ref, wout_ref, o_ref):
    x = x_ref[...]
    h = _modulated_norm(x, nw_ref, sh_ref, sc_ref)
    z = jnp.dot(h, wzg_ref[:, :Z_A], preferred_element_type=F32)
    o = of_ref[...] + ob_ref[...]
    heads = []
    for hh in range(H_A):
        sl = slice(hh * DV_A, (hh + 1) * DV_A)
        heads.append(_rms(o[:, sl], on_ref[...]) * _silu(z[:, sl]))
    y_a = jnp.concatenate(heads, axis=1)
    gate_a = jnp.dot(h, wzg_ref[:, Z_A:Z_A + D_MODEL], preferred_element_type=F32)
    gate_b = jnp.dot(h, wzg_ref[:, Z_A + D_MODEL:], preferred_element_type=F32)
    m = _sigmoid(gate_a) * _dot(y_a, woa_ref[...]) + _sigmoid(gate_b) * _dot(yb_ref[...], wob_ref[...])
    o_ref[...] = x + g2_ref[0, 0] * _dot(m, wout_ref[...])


def _merge(x, o_f, o_b, y_b, mod4, tokens_per_row, row0, norm_w, onorm, w_zg, w_oa, w_ob, w_out):
    n = x.shape[0]
    tm = 512
    tok = lambda w: pl.BlockSpec((tm, w), lambda i: (i, 0))
    return pl.pallas_call(
        _merge_kernel,
        grid=(n // tm,),
        in_specs=[tok(D_MODEL), tok(Z_A), tok(Z_A), tok(Q_B),
                  _mod_spec(3, tm, tokens_per_row, row0),
                  _mod_spec(4, tm, tokens_per_row, row0),
                  _mod_spec(5, tm, tokens_per_row, row0),
                  _const_spec((1, D_MODEL)),
                  _const_spec((1, DV_A)),
                  _const_spec((D_MODEL, Z_A + GATES)),
                  _const_spec((H_A * DV_A, D_MODEL)),
                  _const_spec((H_B * HD_B, D_MODEL)),
                  _const_spec((D_MODEL, D_MODEL))],
        out_specs=tok(D_MODEL),
        out_shape=jax.ShapeDtypeStruct((n, D_MODEL), F32),
        compiler_params=pltpu.CompilerParams(dimension_semantics=("parallel",),
                                             vmem_limit_bytes=VMEM_BIG),
        name="merge",
    )(x, o_f, o_b, y_b, mod4, mod4, mod4, norm_w, onorm, w_zg, w_oa, w_ob, w_out)


def _rope_tables(seq):
    rows = seq // GRID_W
    row = jnp.repeat(jnp.arange(rows, dtype=F32), GRID_W)
    col = jnp.tile(jnp.arange(GRID_W, dtype=F32), rows)
    inv = jnp.power(ROPE_BASE, -jnp.arange(ROPE_AXIS_PAIRS, dtype=F32) / ROPE_AXIS_PAIRS)
    ang = jnp.concatenate([row[:, None] * inv, col[:, None] * inv], axis=-1)
    cos, sin = jnp.cos(ang), jnp.sin(ang)
    cos_t = jnp.tile(cos, (1, LANES // (HD_B // 2)))
    sin_t = jnp.tile(jnp.concatenate([-sin, sin], axis=-1), (1, LANES // HD_B))
    return cos_t, sin_t


def _split_w_in(w_in):
    edges = np.cumsum((0, QKV_A, Z_A, DEC_A, BETA_A, Q_B, K_B, V_B, GATES))
    part = lambda i: w_in[:, int(edges[i]):int(edges[i + 1])].astype(BF)
    pad = jnp.zeros((D_MODEL, LANES - DEC_A - BETA_A), BF)
    w_mix = jnp.concatenate([part(0), part(4), part(5), part(6), part(2), part(3), pad], axis=1)
    w_zg = jnp.concatenate([part(1), part(7)], axis=1)
    return w_mix, w_zg


def _cache_layout(t, nb, seq):
    return t.reshape(nb, KV_B, HD_B, seq).transpose(0, 3, 1, 2)[:, None]


def kernel(x_prompt, x_sample, state_delta_fwd, state_delta_bwd, cache_k, cache_v, c, c_ctx, ada_w, ada_b, norm_ffn1, ffn1_w13, ffn1_w2, norm_mix, w_in, conv_w, a_log, dt_bias, onorm_a, w_oa, w_ob, w_out, sink, norm_ffn2, ffn2_w13, ffn2_w2, norm_final):
    bp, sp = x_prompt.shape[:2]
    bs, ts = x_sample.shape[:2]
    layer = 0

    mod_rows = 16
    cvec = jnp.concatenate([c_ctx[None, :], c, jnp.zeros((mod_rows - 1 - bs, D_MODEL), F32)], axis=0)
    mod4 = _modulation(cvec, ada_w[layer], ada_b[layer]).reshape(mod_rows, N_MOD, 1, D_MODEL)

    row = lambda a: a.reshape(1, -1)
    lane_pad = lambda a: jnp.pad(a.reshape(1, -1), ((0, 0), (0, LANES - a.size)))
    w13_1, w2_1 = ffn1_w13[layer].astype(BF), ffn1_w2[layer].astype(BF)
    w13_2, w2_2 = ffn2_w13[layer].astype(BF), ffn2_w2[layer].astype(BF)
    w_mix, w_zg = _split_w_in(w_in[layer])
    w_oa_b, w_ob_b, w_out_b = w_oa[layer].astype(BF), w_ob[layer].astype(BF), w_out[layer].astype(BF)
    alog_p, dtb_p = lane_pad(a_log[layer]), lane_pad(dt_bias[layer])
    cos_t, sin_t = _rope_tables(ts)
    final_w = row(norm_final)

    def run(x, nb, seq, tokens_per_row, row0, init, ctx):
        latent = ctx is not None
        x1 = _ffn(x, mod4, (0, 1, 2), tokens_per_row, row0, row(norm_ffn1[layer]), w13_1, w2_1,
                  final_w, final=False)
        mix = _inproj(x1, mod4, tokens_per_row, row0, row(norm_mix[layer]), w_mix, cos_t, sin_t,
                      conv_w[layer], alog_p, dtb_p, seq, latent)
        qkvc, qb, kb, vb, gbeta = mix[:5]
        scan = _delta_scan(qkvc, gbeta, init, nb, seq, emit_state=not latent)
        if latent:
            y_b = _window_attention(qb, kb, vb, ctx[0], ctx[1], sink[layer], nb, seq)
        else:
            y_b = _context_attention(qb, kb, vb, sink[layer], nb, seq)
        x2 = _merge(x1, scan[0], scan[1], y_b, mod4, tokens_per_row, row0, row(norm_mix[layer]),
                    row(onorm_a[layer]), w_zg, w_oa_b, w_ob_b, w_out_b)
        y = _ffn(x2, mod4, (6, 7, 8), tokens_per_row, row0, row(norm_ffn2[layer]), w13_2, w2_2,
                 final_w, final=True)
        return y, scan[2:], mix[5:]

    yp, (s_f, s_b), (k_t, v_t) = run(x_prompt.reshape(bp * sp, D_MODEL), bp, sp, bp * sp, 0, None, None)
    ctx = (cache_k[:, layer].reshape(bs, -1, K_B), cache_v[:, layer].reshape(bs, -1, V_B))
    ys, _, _ = run(x_sample.reshape(bs * ts, D_MODEL), bs, ts, ts, 1,
                   (state_delta_fwd[:, layer], state_delta_bwd[:, layer]), ctx)

    return (yp.reshape(bp, sp, D_MODEL), ys.reshape(bs, ts, D_MODEL),
            s_f[:, None], s_b[:, None], _cache_layout(k_t, bp, sp), _cache_layout(v_t, bp, sp))
```

```python
import functools

import jax
import jax.numpy as jnp
import numpy as np
from jax import lax
from jax.experimental import pallas as pl
from jax.experimental.pallas import tpu as pltpu

D_MODEL = 1024
GRID_W = 64
H_A = 4
DK_A = 128
DV_A = 128
CONV_W = 5
H_B = 8
KV_B = 2
HD_B = 64
WINDOW = 128
ROPE_BASE = 10000.0
ROPE_AXIS_PAIRS = HD_B // 4
D_FF = 2816
HALF_STEP = 0.5
N_MOD = 9
EPS = 1e-6

QKV_A = 2 * H_A * DK_A + H_A * DV_A
Z_A = H_A * DV_A
DEC_A = 2 * H_A
BETA_A = 2 * H_A
Q_B = H_B * HD_B
K_B = KV_B * HD_B
V_B = KV_B * HD_B
GATES = 2 * D_MODEL

LANES = 128
SUBLANES = 8
CHUNK = 128
NEG = -1e30

BF = jnp.bfloat16
F32 = jnp.float32
VMEM_BIG = 56 * 1024 * 1024


def _dot(a, b):
    return jnp.dot(a.astype(BF), b.astype(BF), preferred_element_type=F32)


def _dot_nt(a, b):
    return lax.dot_general(a.astype(BF), b.astype(BF), (((1,), (1,)), ((), ())),
                           preferred_element_type=F32)


def _sigmoid(x):
    return 1.0 / (1.0 + jnp.exp(-x))


def _silu(x):
    return x * _sigmoid(x)


def _rms(x, w):
    return x * lax.rsqrt(jnp.mean(x * x, axis=-1, keepdims=True) + EPS) * w


def _modulated_norm(x, nw_ref, sh_ref, sc_ref):
    return (_rms(x, nw_ref[...]) * (1.0 + sc_ref[0, 0]) + sh_ref[0, 0]).astype(BF)


def _const_spec(shape):
    nd = len(shape)
    return pl.BlockSpec(shape, lambda *_: (0,) * nd, pipeline_mode=pl.Buffered(1))


def _mod_spec(kind, tm, tokens_per_row, row0):
    return pl.BlockSpec((1, 1, 1, D_MODEL),
                        lambda i: (row0 + (i * tm) // tokens_per_row, kind, 0, 0))


def _mod_kernel(c_ref, w_ref, b_ref, o_ref):
    o_ref[...] = _dot(_silu(c_ref[...]), w_ref[...]) + b_ref[...]


def _modulation(cvec, ada_w, ada_b):
    rows = cvec.shape[0]
    n = N_MOD * D_MODEL
    bn = 1152
    return pl.pallas_call(
        _mod_kernel,
        grid=(n // bn,),
        in_specs=[pl.BlockSpec((rows, D_MODEL), lambda j: (0, 0)),
                  pl.BlockSpec((D_MODEL, bn), lambda j: (0, j)),
                  pl.BlockSpec((1, bn), lambda j: (0, j))],
        out_specs=pl.BlockSpec((rows, bn), lambda j: (0, j)),
        out_shape=jax.ShapeDtypeStruct((rows, n), F32),
        name="modulation",
    )(cvec, ada_w, ada_b.reshape(1, n))


def _ffn_kernel(x_ref, sh_ref, sc_ref, g_ref, nw_ref, w13_ref, w2_ref, fw_ref, o_ref, *, ff_chunk, final):
    x = x_ref[...]
    h = _modulated_norm(x, nw_ref, sh_ref, sc_ref)
    acc = jnp.zeros(x.shape, F32)
    for j in range(D_FF // ff_chunk):
        a = jnp.dot(h, w13_ref[:, j * ff_chunk:(j + 1) * ff_chunk], preferred_element_type=F32)
        b = jnp.dot(h, w13_ref[:, D_FF + j * ff_chunk:D_FF + (j + 1) * ff_chunk],
                    preferred_element_type=F32)
        t = (_silu(a) * b).astype(BF)
        acc = acc + jnp.dot(t, w2_ref[j * ff_chunk:(j + 1) * ff_chunk, :], preferred_element_type=F32)
    y = x + (HALF_STEP * g_ref[0, 0]) * acc
    if final:
        y = _rms(y, fw_ref[...])
    o_ref[...] = y


def _ffn(x, mod4, kinds, tokens_per_row, row0, norm_w, w13, w2, final_w, final):
    n = x.shape[0]
    tm = 1024
    tok = pl.BlockSpec((tm, D_MODEL), lambda i: (i, 0))
    return pl.pallas_call(
        functools.partial(_ffn_kernel, ff_chunk=256, final=final),
        grid=(n // tm,),
        in_specs=[tok,
                  _mod_spec(kinds[0], tm, tokens_per_row, row0),
                  _mod_spec(kinds[1], tm, tokens_per_row, row0),
                  _mod_spec(kinds[2], tm, tokens_per_row, row0),
                  _const_spec((1, D_MODEL)),
                  _const_spec((D_MODEL, 2 * D_FF)),
                  _const_spec((D_FF, D_MODEL)),
                  _const_spec((1, D_MODEL))],
        out_specs=tok,
        out_shape=jax.ShapeDtypeStruct((n, D_MODEL), F32),
        compiler_params=pltpu.CompilerParams(dimension_semantics=("parallel",),
                                             vmem_limit_bytes=VMEM_BIG),
        name="ffn_final" if final else "ffn",
    )(x, mod4, mod4, mod4, norm_w, w13, w2, final_w)


IN_WIDTH_P = QKV_A + Q_B + K_B + V_B + LANES
IN_TILE_MAX = 512
HALO = SUBLANES


def _swap_halves(x):
    w = x.shape[-1]
    lane = lax.broadcasted_iota(jnp.int32, x.shape, 1)
    lower = (lane % HD_B) < (HD_B // 2)
    return jnp.where(lower, pltpu.roll(x, w - HD_B // 2, 1), pltpu.roll(x, HD_B // 2, 1))


def _rope(p, cos_ref, sin_ref):
    reps = p.shape[1] // LANES
    c = jnp.concatenate([cos_ref[...]] * reps, axis=1) if reps > 1 else cos_ref[...]
    s = jnp.concatenate([sin_ref[...]] * reps, axis=1) if reps > 1 else sin_ref[...]
    return p * c + _swap_halves(p) * s


def _inproj_kernel(xm_ref, xp_ref, xn_ref, sh_ref, sc_ref, nw_ref, w_ref, cos_ref, sin_ref,
                   cw_ref, alog_ref, dtb_ref, qkv_ref, qb_ref, kb_ref, vb_ref, gbeta_ref, *rest,
                   latent, tiles):
    t = pl.program_id(0) % tiles
    x = jnp.concatenate([xp_ref[...], xm_ref[...], xn_ref[...]], axis=0)
    h = _modulated_norm(x, nw_ref, sh_ref, sc_ref)
    rows = x.shape[0]
    tile = rows - 2 * HALO
    blocks = rows // SUBLANES
    rid = lax.broadcasted_iota(jnp.int32, (rows, 1), 0)
    outside = ((rid < HALO) & (t == 0)) | ((rid >= HALO + tile) & (t == tiles - 1))
    hm = h[HALO:HALO + tile]

    def project(first, width, lhs=hm):
        return jnp.dot(lhs, w_ref[:, first:first + width], preferred_element_type=F32)

    def conv_section(raw, first):
        width = raw.shape[1]
        r3 = jnp.where(outside, 0.0, raw).reshape(blocks, SUBLANES, width)
        sub = lax.broadcasted_iota(jnp.int32, (1, SUBLANES, width), 1)
        acc = None
        for j in range(CONV_W):
            s = CONV_W // 2 - j
            tap = cw_ref[j:j + 1, first:first + width].reshape(1, 1, width)
            if s == 0:
                sh = r3[1:blocks - 1]
            else:
                rot = pltpu.roll(r3, s % SUBLANES, 1)
                if s > 0:
                    sh = jnp.where(sub >= s, rot[1:blocks - 1], rot[0:blocks - 2])
                else:
                    sh = jnp.where(sub < SUBLANES + s, rot[1:blocks - 1], rot[2:blocks])
            acc = sh * tap if acc is None else acc + sh * tap
        y = _silu(acc.reshape(tile, width))
        for hh in range(width // DK_A):
            v = y[:, hh * DK_A:(hh + 1) * DK_A]
            if first < 2 * H_A * DK_A:
                scale = DK_A ** -0.5 if first < H_A * DK_A else 1.0
                v = v * (lax.rsqrt(jnp.sum(v * v, axis=-1, keepdims=True) + EPS) * scale)
            qkv_ref[:, first + hh * DK_A:first + (hh + 1) * DK_A] = v

    sec = H_A * DK_A
    qb0 = QKV_A
    kv0 = qb0 + Q_B
    db0 = kv0 + K_B + V_B
    for i in range(QKV_A // sec):
        conv_section(project(i * sec, sec, h), i * sec)
    qb = project(qb0, Q_B)
    qb_ref[...] = _rope(qb, cos_ref, sin_ref) if latent else qb
    kv = project(kv0, K_B + V_B)
    kb_ref[...] = _rope(kv[:, :K_B], cos_ref, sin_ref) if latent else kv[:, :K_B]
    vb_ref[...] = kv[:, K_B:]
    if not latent:
        kt_ref, vt_ref = rest
        kt_ref[0] = kv[:, :K_B].T
        vt_ref[0] = kv[:, K_B:].T
    db = project(db0, LANES)

    xg = db + dtb_ref[...]
    softplus = jnp.maximum(xg, 0.0) + jnp.log1p(jnp.exp(-jnp.abs(xg)))
    g = -jnp.exp(alog_ref[...]) * softplus
    lane = lax.broadcasted_iota(jnp.int32, db.shape, 1)
    gbeta_ref[...] = jnp.where(lane < DEC_A, g, jnp.where(lane < DEC_A + BETA_A, _sigmoid(db), 0.0))


def _inproj(x, mod4, tokens_per_row, row0, norm_w, w_mix, cos_t, sin_t, conv_w, alog_p, dtb_p, seq, latent):
    n = x.shape[0]
    tm = min(IN_TILE_MAX, seq)
    tiles = seq // tm
    halos_per_tile = tm // HALO
    last_halo = n // HALO - 1
    tok = lambda w: pl.BlockSpec((tm, w), lambda i: (i, 0))
    prev = pl.BlockSpec((HALO, D_MODEL), lambda i: (jnp.maximum(i * halos_per_tile - 1, 0), 0))
    nxt = pl.BlockSpec((HALO, D_MODEL), lambda i: (jnp.minimum((i + 1) * halos_per_tile, last_halo), 0))
    tab = pl.BlockSpec((tm, LANES), lambda i: (i % tiles, 0))
    widths = (QKV_A, Q_B, K_B, V_B, LANES)
    out_specs = [tok(w) for w in widths]
    out_shape = [jax.ShapeDtypeStruct((n, w), F32) for w in widths]
    if not latent:
        assert tiles == 1
        out_specs += [pl.BlockSpec((1, K_B, tm), lambda i: (i, 0, 0))] * 2
        out_shape += [jax.ShapeDtypeStruct((n // tm, K_B, tm), F32)] * 2
    return pl.pallas_call(
        functools.partial(_inproj_kernel, latent=latent, tiles=tiles),
        grid=(n // tm,),
        in_specs=[tok(D_MODEL), prev, nxt,
                  _mod_spec(3, tm, tokens_per_row, row0),
                  _mod_spec(4, tm, tokens_per_row, row0),
                  _const_spec((1, D_MODEL)),
                  _const_spec((D_MODEL, IN_WIDTH_P)),
                  tab, tab,
                  _const_spec((CONV_W, QKV_A)),
                  _const_spec((1, LANES)),
                  _const_spec((1, LANES))],
        out_specs=out_specs,
        out_shape=out_shape,
        compiler_params=pltpu.CompilerParams(dimension_semantics=("parallel",),
                                             vmem_limit_bytes=VMEM_BIG),
        name="inproj_latent" if latent else "inproj_context",
    )(x, x, x, mod4, mod4, norm_w, w_mix, cos_t, sin_t, conv_w, alog_p, dtb_p)


TRI_BASE = 16
SCAN_CHUNKS_PER_STEP = 4


def _unit_tri_inverses(lows):
    n = lows[0].shape[0]
    row = lax.broadcasted_iota(jnp.int32, (n, n), 0)
    col = lax.broadcasted_iota(jnp.int32, (n, n), 1)
    base = row // TRI_BASE == col // TRI_BASE
    eye = jnp.where(row == col, 1.0, 0.0)
    ps = [-jnp.where(base, low, 0.0) for low in lows]
    ts = [eye + p for p in ps]
    ps = [_dot(p, p) for p in ps]
    for _ in range(int(np.log2(TRI_BASE)) - 2):
        both = [_dot(jnp.concatenate([p.astype(BF), t.astype(BF)], axis=0), p) for p, t in zip(ps, ts)]
        ts = [t + b[n:] for t, b in zip(ts, both)]
        ps = [b[:n] for b in both]
    ts = [t + _dot(t, p) for t, p in zip(ts, ps)]
    size = TRI_BASE
    while size < n:
        pair = (row // (2 * size) == col // (2 * size)) & (row // size != col // size)
        mids = [_dot(jnp.where(pair, low, 0.0), t) for low, t in zip(lows, ts)]
        ts = [t - _dot(t, m) for t, m in zip(ts, mids)]
        size *= 2
    return ts


def _scan_kernel(*refs, steps, chunks, has_init, emit_state):
    xf_ref, xb_ref, gf_ref, gb_ref = refs[:4]
    rest = list(refs[4:])
    s0_refs = (rest.pop(0), rest.pop(0)) if has_init else None
    of_ref, ob_ref = rest.pop(0), rest.pop(0)
    out_state_refs = (rest.pop(0), rest.pop(0)) if emit_state else None
    state, = rest
    c = pl.program_id(1)

    @pl.when(c == 0)
    def _():
        if has_init:
            state[0:H_A] = s0_refs[0][0]
            state[H_A:2 * H_A] = s0_refs[1][0]
        else:
            state[...] = jnp.zeros(state.shape, F32)

    row = lax.broadcasted_iota(jnp.int32, (CHUNK, CHUNK), 0)
    col = lax.broadcasted_iota(jnp.int32, (CHUNK, CHUNK), 1)
    x_refs, g_refs, o_refs = (xf_ref, xb_ref), (gf_ref, gb_ref), (of_ref, ob_ref)
    incl = (row >= col, row <= col)
    strict = (row > col, row < col)
    chains = [(d, hh) for d in range(2) for hh in range(H_A)]
    nch = len(chains)

    def one_chunk(sub, carry):
        rows = (pl.ds(pl.multiple_of(sub * CHUNK, CHUNK), CHUNK),
                pl.ds(pl.multiple_of((chunks - 1 - sub) * CHUNK, CHUNK), CHUNK))
        gbeta = [g_refs[d][rows[d], :] for d in range(2)]
        gc = [jnp.dot(jnp.where(incl[d], 1.0, 0.0), gbeta[d], precision=lax.Precision.HIGHEST,
                      preferred_element_type=F32) for d in range(2)]
        gct = [g.T for g in gc]
        q = [x_refs[d][rows[d], hh * DK_A:(hh + 1) * DK_A] for d, hh in chains]
        k = [x_refs[d][rows[d], (H_A + hh) * DK_A:(H_A + hh + 1) * DK_A] for d, hh in chains]
        v = [x_refs[d][rows[d], (2 * H_A + hh) * DK_A:(2 * H_A + hh + 1) * DK_A] for d, hh in chains]
        gcol = [gc[d][:, d * H_A + hh:d * H_A + hh + 1] for d, hh in chains]
        beta = [gbeta[d][:, DEC_A + d * H_A + hh:DEC_A + d * H_A + hh + 1] for d, hh in chains]
        decay = [jnp.exp(jnp.where(incl[d], gcol[i] - gct[d][d * H_A + hh:d * H_A + hh + 1, :], NEG))
                 for i, (d, hh) in enumerate(chains)]
        qk_kk = [_dot_nt(jnp.concatenate([q[i].astype(BF), k[i].astype(BF)], axis=0), k[i]) for i in range(nch)]
        tmat = _unit_tri_inverses([jnp.where(strict[d], qk_kk[i][CHUNK:] * beta[i] * decay[i], 0.0)
                                   for i, (d, hh) in enumerate(chains)])
        egc = [jnp.exp(g) for g in gcol]
        uw = [_dot(tmat[i], jnp.concatenate([v[i] * beta[i], k[i] * (beta[i] * egc[i])], axis=1))
              for i in range(nch)]
        g_last = [gcol[i][CHUNK - 1:CHUNK] if d == 0 else gcol[i][0:1] for i, (d, hh) in enumerate(chains)]
        s_old = [state[i] for i in range(nch)]
        s_bf = [s.astype(BF) for s in s_old]
        ws_qs = [_dot(jnp.concatenate([uw[i][:, DV_A:].astype(BF), (q[i] * egc[i]).astype(BF)], axis=0), s_bf[i])
                 for i in range(nch)]
        v_new = [uw[i][:, :DV_A] - ws_qs[i][:CHUNK] for i in range(nch)]
        kd_t = [(k[i] * jnp.exp(g_last[i] - gcol[i])).T for i in range(nch)]
        av_kv = [_dot(jnp.concatenate([(qk_kk[i][:CHUNK] * decay[i]).astype(BF), kd_t[i].astype(BF)], axis=0),
                      v_new[i]) for i in range(nch)]
        for i, (d, hh) in enumerate(chains):
            o_refs[d][rows[d], hh * DV_A:(hh + 1) * DV_A] = ws_qs[i][CHUNK:] + av_kv[i][:CHUNK]
        for i in range(nch):
            state[i] = s_old[i] * jnp.exp(g_last[i]) + av_kv[i][CHUNK:]
        return carry

    lax.fori_loop(0, chunks, one_chunk, 0, unroll=True)

    if emit_state:
        @pl.when(c == steps - 1)
        def _():
            out_state_refs[0][0] = state[0:H_A]
            out_state_refs[1][0] = state[H_A:2 * H_A]


def _delta_scan(qkvc, gbeta, init, nb, seq, emit_state):
    n = qkvc.shape[0]
    chunks = min(SCAN_CHUNKS_PER_STEP, seq // CHUNK)
    rows = chunks * CHUNK
    steps = seq // rows
    fwd = lambda w: pl.BlockSpec((rows, w), lambda b, c: (b * steps + c, 0))
    bwd = lambda w: pl.BlockSpec((rows, w), lambda b, c: (b * steps + steps - 1 - c, 0))
    st = pl.BlockSpec((1, H_A, DK_A, DV_A), lambda b, c: (b, 0, 0, 0))
    st_shape = jax.ShapeDtypeStruct((nb, H_A, DK_A, DV_A), F32)
    o_shape = jax.ShapeDtypeStruct((n, Z_A), F32)
    has_init = init is not None
    return pl.pallas_call(
        functools.partial(_scan_kernel, steps=steps, chunks=chunks, has_init=has_init, emit_state=emit_state),
        grid=(nb, steps),
        in_specs=[fwd(QKV_A), bwd(QKV_A), fwd(LANES), bwd(LANES)] + ([st, st] if has_init else []),
        out_specs=[fwd(Z_A), bwd(Z_A)] + ([st, st] if emit_state else []),
        out_shape=[o_shape, o_shape] + ([st_shape, st_shape] if emit_state else []),
        scratch_shapes=[pltpu.VMEM((2 * H_A, DK_A, DV_A), F32)],
        compiler_params=pltpu.CompilerParams(dimension_semantics=("parallel", "arbitrary")),
        name="delta_scan",
    )(qkvc, qkvc, gbeta, gbeta, *(init if has_init else ()))


def _attn_core(q, keys, vals, valid, sink_ref, write):
    nq = q.shape[0]
    group = H_B // KV_B
    klane = lax.broadcasted_iota(jnp.int32, keys.shape, 1)
    krot = pltpu.roll(keys, HD_B, 1)
    qlow = lax.broadcasted_iota(jnp.int32, (nq, LANES), 1) < HD_B
    v_bf = vals.astype(BF)
    kdup = [jnp.where(klane < HD_B, keys, krot).astype(BF), jnp.where(klane < HD_B, krot, keys).astype(BF)]
    scores = []
    for head in range(H_B):
        qt = q[:, (head // 2) * LANES:(head // 2 + 1) * LANES]
        qh = jnp.where(qlow, qt, 0.0) if head % 2 == 0 else jnp.where(qlow, 0.0, qt)
        scores.append(_dot_nt(qh, kdup[head // group]))
    probs, inv_den = [], []
    for head, s in enumerate(scores):
        if valid is not None:
            s = jnp.where(valid, s, NEG)
        snk = sink_ref[head]
        m = jnp.maximum(jnp.max(s, axis=-1, keepdims=True), snk)
        p = jnp.exp(s - m)
        inv_den.append(1.0 / (jnp.sum(p, axis=-1, keepdims=True) + jnp.exp(snk - m)))
        probs.append(p.astype(BF))
    o_all = [jnp.dot(jnp.concatenate(probs[g * group:(g + 1) * group], axis=0), v_bf,
                     preferred_element_type=F32) for g in range(KV_B)]
    for g in range(KV_B):
        for t in range(group // 2):
            a = o_all[g][(2 * t) * nq:(2 * t + 1) * nq] * inv_den[g * group + 2 * t]
            b = o_all[g][(2 * t + 1) * nq:(2 * t + 2) * nq] * inv_den[g * group + 2 * t + 1]
            if g == 0:
                tile = jnp.where(qlow, a, pltpu.roll(b, HD_B, 1))
            else:
                tile = jnp.where(qlow, pltpu.roll(a, HD_B, 1), b)
            write((g * group // 2 + t) * LANES, tile)


def _ctx_attn_kernel(sink_ref, q_ref, k_ref, v_ref, o_ref):
    def write(col, tile):
        o_ref[:, col:col + LANES] = tile

    _attn_core(q_ref[...] * HD_B ** -0.5, k_ref[...], v_ref[...], None, sink_ref, write)


def _context_attention(q, k, v, sink, nb, seq):
    n = q.shape[0]
    blk = lambda w: pl.BlockSpec((seq, w), lambda b: (b, 0))
    return pl.pallas_call(
        _ctx_attn_kernel,
        grid=(nb,),
        in_specs=[pl.BlockSpec(memory_space=pltpu.SMEM), blk(Q_B), blk(K_B), blk(V_B)],
        out_specs=blk(Q_B),
        out_shape=jax.ShapeDtypeStruct((n, Q_B), F32),
        compiler_params=pltpu.CompilerParams(dimension_semantics=("parallel",)),
        name="context_attention",
    )(sink, q, k, v)


WIN_SUB = 4


def _win_attn_kernel(sink_ref, q_ref, k_ref, v_ref, kx_ref, vx_ref, o_ref):
    step = pl.program_id(1)
    seq = k_ref.shape[0]
    n_local = 3 * WINDOW
    nk = n_local + kx_ref.shape[1]
    r = lax.broadcasted_iota(jnp.int32, (WINDOW, nk), 0)
    j = lax.broadcasted_iota(jnp.int32, (WINDOW, nk), 1)

    def block(sub, carry):
        i = step * WIN_SUB + sub
        start = pl.multiple_of(jnp.clip((i - 1) * WINDOW, 0, seq - n_local), WINDOW)
        keys = jnp.concatenate([k_ref[pl.ds(start, n_local), :], kx_ref[0]], axis=0)
        vals = jnp.concatenate([v_ref[pl.ds(start, n_local), :], vx_ref[0]], axis=0)
        dist = (i * WINDOW - start) + r - j
        valid = ((dist <= WINDOW) & (dist >= -WINDOW)) | (j >= n_local)
        rows = pl.ds(pl.multiple_of(sub * WINDOW, WINDOW), WINDOW)

        def write(col, tile):
            o_ref[rows, col:col + LANES] = tile

        _attn_core(q_ref[rows, :] * HD_B ** -0.5, keys, vals, valid, sink_ref, write)
        return carry

    lax.fori_loop(0, WIN_SUB, block, 0, unroll=True)


def _window_attention(q, k, v, k_ctx, v_ctx, sink, nb, seq):
    n = q.shape[0]
    tq = WIN_SUB * WINDOW
    steps = seq // tq
    n_ctx = k_ctx.shape[1]
    qblk = pl.BlockSpec((tq, Q_B), lambda b, i: (b * steps + i, 0))
    kv = pl.BlockSpec((seq, K_B), lambda b, i: (b, 0))
    ctx = pl.BlockSpec((1, n_ctx, K_B), lambda b, i: (b, 0, 0))
    return pl.pallas_call(
        _win_attn_kernel,
        grid=(nb, steps),
        in_specs=[pl.BlockSpec(memory_space=pltpu.SMEM), qblk, kv, kv, ctx, ctx],
        out_specs=qblk,
        out_shape=jax.ShapeDtypeStruct((n, Q_B), F32),
        compiler_params=pltpu.CompilerParams(dimension_semantics=("parallel", "parallel")),
        name="window_attention",
    )(sink, q, k, v, k_ctx, v_ctx)


def _merge_kernel(x_ref, of_ref, ob_ref, yb_ref, sh_ref, sc_ref, g2_ref, nw_ref, on_ref,
                  wzg_ref, woa_ref, wob_ref, wout_ref, o_ref):
    x = x_ref[...]
    h = _modulated_norm(x, nw_ref, sh_ref, sc_ref)
    z = jnp.dot(h, wzg_ref[:, :Z_A], preferred_element_type=F32)
    o = of_ref[...] + ob_ref[...]
    heads = []
    for hh in range(H_A):
        sl = slice(hh * DV_A, (hh + 1) * DV_A)
        heads.append(_rms(o[:, sl], on_ref[...]) * _silu(z[:, sl]))
    y_a = jnp.concatenate(heads, axis=1)
    gate_a = jnp.dot(h, wzg_ref[:, Z_A:Z_A + D_MODEL], preferred_element_type=F32)
    gate_b = jnp.dot(h, wzg_ref[:, Z_A + D_MODEL:], preferred_element_type=F32)
    m = _sigmoid(gate_a) * _dot(y_a, woa_ref[...]) + _sigmoid(gate_b) * _dot(yb_ref[...], wob_ref[...])
    o_ref[...] = x + g2_ref[0, 0] * _dot(m, wout_ref[...])


def _merge(x, o_f, o_b, y_b, mod4, tokens_per_row, row0, norm_w, onorm, w_zg, w_oa, w_ob, w_out):
    n = x.shape[0]
    tm = 512
    tok = lambda w: pl.BlockSpec((tm, w), lambda i: (i, 0))
    return pl.pallas_call(
        _merge_kernel,
        grid=(n // tm,),
        in_specs=[tok(D_MODEL), tok(Z_A), tok(Z_A), tok(Q_B),
                  _mod_spec(3, tm, tokens_per_row, row0),
                  _mod_spec(4, tm, tokens_per_row, row0),
                  _mod_spec(5, tm, tokens_per_row, row0),
                  _const_spec((1, D_MODEL)),
                  _const_spec((1, DV_A)),
                  _const_spec((D_MODEL, Z_A + GATES)),
                  _const_spec((H_A * DV_A, D_MODEL)),
                  _const_spec((H_B * HD_B, D_MODEL)),
                  _const_spec((D_MODEL, D_MODEL))],
        out_specs=tok(D_MODEL),
        out_shape=jax.ShapeDtypeStruct((n, D_MODEL), F32),
        compiler_params=pltpu.CompilerParams(dimension_semantics=("parallel",),
                                             vmem_limit_bytes=VMEM_BIG),
        name="merge",
    )(x, o_f, o_b, y_b, mod4, mod4, mod4, norm_w, onorm, w_zg, w_oa, w_ob, w_out)


def _rope_tables(seq):
    rows = seq // GRID_W
    row = jnp.repeat(jnp.arange(rows, dtype=F32), GRID_W)
    col = jnp.tile(jnp.arange(GRID_W, dtype=F32), rows)
    inv = jnp.power(ROPE_BASE, -jnp.arange(ROPE_AXIS_PAIRS, dtype=F32) / ROPE_AXIS_PAIRS)
    ang = jnp.concatenate([row[:, None] * inv, col[:, None] * inv], axis=-1)
    cos, sin = jnp.cos(ang), jnp.sin(ang)
    cos_t = jnp.tile(cos, (1, LANES // (HD_B // 2)))
    sin_t = jnp.tile(jnp.concatenate([-sin, sin], axis=-1), (1, LANES // HD_B))
    return cos_t, sin_t


def _split_w_in(w_in):
    edges = np.cumsum((0, QKV_A, Z_A, DEC_A, BETA_A, Q_B, K_B, V_B, GATES))
    part = lambda i: w_in[:, int(edges[i]):int(edges[i + 1])].astype(BF)
    pad = jnp.zeros((D_MODEL, LANES - DEC_A - BETA_A), BF)
    w_mix = jnp.concatenate([part(0), part(4), part(5), part(6), part(2), part(3), pad], axis=1)
    w_zg = jnp.concatenate([part(1), part(7)], axis=1)
    return w_mix, w_zg


def _cache_layout(t, nb, seq):
    return t.reshape(nb, KV_B, HD_B, seq).transpose(0, 3, 1, 2)[:, None]


def kernel(x_prompt, x_sample, state_delta_fwd, state_delta_bwd, cache_k, cache_v, c, c_ctx, ada_w, ada_b, norm_ffn1, ffn1_w13, ffn1_w2, norm_mix, w_in, conv_w, a_log, dt_bias, onorm_a, w_oa, w_ob, w_out, sink, norm_ffn2, ffn2_w13, ffn2_w2, norm_final):
    bp, sp = x_prompt.shape[:2]
    bs, ts = x_sample.shape[:2]
    layer = 0

    mod_rows = 16
    cvec = jnp.concatenate([c_ctx[None, :], c, jnp.zeros((mod_rows - 1 - bs, D_MODEL), F32)], axis=0)
    mod4 = _modulation(cvec, ada_w[layer], ada_b[layer]).reshape(mod_rows, N_MOD, 1, D_MODEL)

    row = lambda a: a.reshape(1, -1)
    lane_pad = lambda a: jnp.pad(a.reshape(1, -1), ((0, 0), (0, LANES - a.size)))
    w13_1, w2_1 = ffn1_w13[layer].astype(BF), ffn1_w2[layer].astype(BF)
    w13_2, w2_2 = ffn2_w13[layer].astype(BF), ffn2_w2[layer].astype(BF)
    w_mix, w_zg = _split_w_in(w_in[layer])
    w_oa_b, w_ob_b, w_out_b = w_oa[layer].astype(BF), w_ob[layer].astype(BF), w_out[layer].astype(BF)
    alog_p, dtb_p = lane_pad(a_log[layer]), lane_pad(dt_bias[layer])
    cos_t, sin_t = _rope_tables(ts)
    final_w = row(norm_final)

    def run(x, nb, seq, tokens_per_row, row0, init, ctx):
        latent = ctx is not None
        x1 = _ffn(x, mod4, (0, 1, 2), tokens_per_row, row0, row(norm_ffn1[layer]), w13_1, w2_1,
                  final_w, final=False)
        mix = _inproj(x1, mod4, tokens_per_row, row0, row(norm_mix[layer]), w_mix, cos_t, sin_t,
                      conv_w[layer], alog_p, dtb_p, seq, latent)
        qkvc, qb, kb, vb, gbeta = mix[:5]
        scan = _delta_scan(qkvc, gbeta, init, nb, seq, emit_state=not latent)
        if latent:
            y_b = _window_attention(qb, kb, vb, ctx[0], ctx[1], sink[layer], nb, seq)
        else:
            y_b = _context_attention(qb, kb, vb, sink[layer], nb, seq)
        x2 = _merge(x1, scan[0], scan[1], y_b, mod4, tokens_per_row, row0, row(norm_mix[layer]),
                    row(onorm_a[layer]), w_zg, w_oa_b, w_ob_b, w_out_b)
        y = _ffn(x2, mod4, (6, 7, 8), tokens_per_row, row0, row(norm_ffn2[layer]), w13_2, w2_2,
                 final_w, final=True)
        return y, scan[2:], mix[5:]

    yp, (s_f, s_b), (k_t, v_t) = run(x_prompt.reshape(bp * sp, D_MODEL), bp, sp, bp * sp, 0, None, None)
    ctx = (cache_k[:, layer].reshape(bs, -1, K_B), cache_v[:, layer].reshape(bs, -1, V_B))
    ys, _, _ = run(x_sample.reshape(bs * ts, D_MODEL), bs, ts, ts, 1,
                   (state_delta_fwd[:, layer], state_delta_bwd[:, layer]), ctx)

    return (yp.reshape(bp, sp, D_MODEL), ys.reshape(bs, ts, D_MODEL),
            s_f[:, None], s_b[:, None], _cache_layout(k_t, bp, sp), _cache_layout(v_t, bp, sp))
```

```python
import functools

import jax
import jax.numpy as jnp
import numpy as np
from jax import lax
from jax.experimental import pallas as pl
from jax.experimental.pallas import tpu as pltpu

D_MODEL = 1024
GRID_W = 64
H_A = 4
DK_A = 128
DV_A = 128
CONV_W = 5
H_B = 8
KV_B = 2
HD_B = 64
WINDOW = 128
ROPE_BASE = 10000.0
ROPE_AXIS_PAIRS = HD_B // 4
D_FF = 2816
HALF_STEP = 0.5
N_MOD = 9
EPS = 1e-6

QKV_A = 2 * H_A * DK_A + H_A * DV_A
Z_A = H_A * DV_A
DEC_A = 2 * H_A
BETA_A = 2 * H_A
Q_B = H_B * HD_B
K_B = KV_B * HD_B
V_B = KV_B * HD_B
GATES = 2 * D_MODEL

LANES = 128
SUBLANES = 8
CHUNK = 128
NEG = -1e30

BF = jnp.bfloat16
F32 = jnp.float32
VMEM_BIG = 56 * 1024 * 1024


def _dot(a, b):
    return jnp.dot(a.astype(BF), b.astype(BF), preferred_element_type=F32)


def _dot_nt(a, b):
    return lax.dot_general(a.astype(BF), b.astype(BF), (((1,), (1,)), ((), ())),
                           preferred_element_type=F32)


def _sigmoid(x):
    return 1.0 / (1.0 + jnp.exp(-x))


def _silu(x):
    return x * _sigmoid(x)


def _rms(x, w):
    return x * lax.rsqrt(jnp.mean(x * x, axis=-1, keepdims=True) + EPS) * w


def _modulated_norm(x, nw_ref, sh_ref, sc_ref):
    gain = nw_ref[...] * (1.0 + sc_ref[0, 0])
    xn = x * lax.rsqrt(jnp.mean(x * x, axis=-1, keepdims=True) + EPS)
    return (xn * gain + sh_ref[0, 0]).astype(BF)


def _const_spec(shape):
    nd = len(shape)
    return pl.BlockSpec(shape, lambda *_: (0,) * nd, pipeline_mode=pl.Buffered(1))


def _mod_spec(kind, tm, tokens_per_row, row0):
    return pl.BlockSpec((1, 1, 1, D_MODEL),
                        lambda i: (row0 + (i * tm) // tokens_per_row, kind, 0, 0))


def _mod_kernel(c_ref, w_ref, b_ref, o_ref):
    o_ref[...] = _dot(_silu(c_ref[...]), w_ref[...]) + b_ref[...]


def _modulation(cvec, ada_w, ada_b):
    rows = cvec.shape[0]
    n = N_MOD * D_MODEL
    bn = 1152
    return pl.pallas_call(
        _mod_kernel,
        grid=(n // bn,),
        in_specs=[pl.BlockSpec((rows, D_MODEL), lambda j: (0, 0)),
                  pl.BlockSpec((D_MODEL, bn), lambda j: (0, j)),
                  pl.BlockSpec((1, bn), lambda j: (0, j))],
        out_specs=pl.BlockSpec((rows, bn), lambda j: (0, j)),
        out_shape=jax.ShapeDtypeStruct((rows, n), F32),
        name="modulation",
    )(cvec, ada_w, ada_b.reshape(1, n))


def _ffn_kernel(x_ref, sh_ref, sc_ref, g_ref, nw_ref, w13_ref, w2_ref, fw_ref, o_ref, *, ff_chunk, final):
    x = x_ref[...]
    h = _modulated_norm(x, nw_ref, sh_ref, sc_ref)
    acc = jnp.zeros(x.shape, F32)
    for j in range(D_FF // ff_chunk):
        a = jnp.dot(h, w13_ref[:, j * ff_chunk:(j + 1) * ff_chunk], preferred_element_type=F32)
        b = jnp.dot(h, w13_ref[:, D_FF + j * ff_chunk:D_FF + (j + 1) * ff_chunk],
                    preferred_element_type=F32)
        t = (_silu(a) * b).astype(BF)
        acc = acc + jnp.dot(t, w2_ref[j * ff_chunk:(j + 1) * ff_chunk, :], preferred_element_type=F32)
    y = x + (HALF_STEP * g_ref[0, 0]) * acc
    if final:
        y = _rms(y, fw_ref[...])
    o_ref[...] = y


def _ffn(x, mod4, kinds, tokens_per_row, row0, norm_w, w13, w2, final_w, final):
    n = x.shape[0]
    tm = 1024
    tok = pl.BlockSpec((tm, D_MODEL), lambda i: (i, 0))
    return pl.pallas_call(
        functools.partial(_ffn_kernel, ff_chunk=256, final=final),
        grid=(n // tm,),
        in_specs=[tok,
                  _mod_spec(kinds[0], tm, tokens_per_row, row0),
                  _mod_spec(kinds[1], tm, tokens_per_row, row0),
                  _mod_spec(kinds[2], tm, tokens_per_row, row0),
                  _const_spec((1, D_MODEL)),
                  _const_spec((D_MODEL, 2 * D_FF)),
                  _const_spec((D_FF, D_MODEL)),
                  _const_spec((1, D_MODEL))],
        out_specs=tok,
        out_shape=jax.ShapeDtypeStruct((n, D_MODEL), F32),
        compiler_params=pltpu.CompilerParams(dimension_semantics=("parallel",),
                                             vmem_limit_bytes=VMEM_BIG),
        name="ffn_final" if final else "ffn",
    )(x, mod4, mod4, mod4, norm_w, w13, w2, final_w)


IN_WIDTH_P = QKV_A + Q_B + K_B + V_B + LANES
IN_TILE_MAX = 512
HALO = SUBLANES


def _swap_halves(x):
    w = x.shape[-1]
    lane = lax.broadcasted_iota(jnp.int32, x.shape, 1)
    lower = (lane % HD_B) < (HD_B // 2)
    return jnp.where(lower, pltpu.roll(x, w - HD_B // 2, 1), pltpu.roll(x, HD_B // 2, 1))


def _rope(p, cos_ref, sin_ref):
    reps = p.shape[1] // LANES
    c = jnp.concatenate([cos_ref[...]] * reps, axis=1) if reps > 1 else cos_ref[...]
    s = jnp.concatenate([sin_ref[...]] * reps, axis=1) if reps > 1 else sin_ref[...]
    return p * c + _swap_halves(p) * s


def _inproj_kernel(xm_ref, xp_ref, xn_ref, sh_ref, sc_ref, nw_ref, w_ref, cos_ref, sin_ref,
                   cw_ref, alog_ref, dtb_ref, qkv_ref, qb_ref, kb_ref, vb_ref, gbeta_ref, *rest,
                   latent, tiles):
    t = pl.program_id(0) % tiles
    x = jnp.concatenate([xp_ref[...], xm_ref[...], xn_ref[...]], axis=0)
    h = _modulated_norm(x, nw_ref, sh_ref, sc_ref)
    rows = x.shape[0]
    tile = rows - 2 * HALO
    blocks = rows // SUBLANES
    hm = h[HALO:HALO + tile]

    def project(first, width, lhs=hm):
        return jnp.dot(lhs, w_ref[:, first:first + width], preferred_element_type=F32)

    def conv_section(raw, first):
        width = raw.shape[1]
        r3 = raw.reshape(blocks, SUBLANES, width)
        r3 = jnp.concatenate([jnp.where(t == 0, 0.0, r3[0:1]), r3[1:blocks - 1],
                              jnp.where(t == tiles - 1, 0.0, r3[blocks - 1:blocks])], axis=0)
        sub = lax.broadcasted_iota(jnp.int32, (1, SUBLANES, width), 1)
        acc = None
        for j in range(CONV_W):
            s = CONV_W // 2 - j
            tap = cw_ref[j:j + 1, first:first + width].reshape(1, 1, width)
            if s == 0:
                sh = r3[1:blocks - 1]
            else:
                if s > 0:
                    m = jnp.where(sub < SUBLANES - s, r3[1:blocks - 1], r3[0:blocks - 2])
                else:
                    m = jnp.where(sub >= -s, r3[1:blocks - 1], r3[2:blocks])
                sh = pltpu.roll(m, s % SUBLANES, 1)
            acc = sh * tap if acc is None else acc + sh * tap
        y = _silu(acc.reshape(tile, width))
        for hh in range(width // DK_A):
            v = y[:, hh * DK_A:(hh + 1) * DK_A]
            if first < 2 * H_A * DK_A:
                scale = DK_A ** -0.5 if first < H_A * DK_A else 1.0
                v = v * (lax.rsqrt(jnp.sum(v * v, axis=-1, keepdims=True) + EPS) * scale)
            qkv_ref[:, first + hh * DK_A:first + (hh + 1) * DK_A] = v

    sec = H_A * DK_A
    qb0 = QKV_A
    kv0 = qb0 + Q_B
    db0 = kv0 + K_B + V_B
    for i in range(QKV_A // sec):
        conv_section(project(i * sec, sec, h), i * sec)
    qb = project(qb0, Q_B)
    qb_ref[...] = _rope(qb, cos_ref, sin_ref) if latent else qb
    kv = project(kv0, K_B + V_B)
    kb_ref[...] = _rope(kv[:, :K_B], cos_ref, sin_ref) if latent else kv[:, :K_B]
    vb_ref[...] = kv[:, K_B:]
    if not latent:
        kt_ref, vt_ref = rest
        kt_ref[0] = kv[:, :K_B].T
        vt_ref[0] = kv[:, K_B:].T
    db = project(db0, LANES)

    xg = db + dtb_ref[...]
    softplus = jnp.maximum(xg, 0.0) + jnp.log1p(jnp.exp(-jnp.abs(xg)))
    g = -jnp.exp(alog_ref[...]) * softplus
    lane = lax.broadcasted_iota(jnp.int32, db.shape, 1)
    gbeta_ref[...] = jnp.where(lane < DEC_A, g, jnp.where(lane < DEC_A + BETA_A, _sigmoid(db), 0.0))


def _inproj(x, mod4, tokens_per_row, row0, norm_w, w_mix, cos_t, sin_t, conv_w, alog_p, dtb_p, seq, latent):
    n = x.shape[0]
    tm = min(IN_TILE_MAX, seq)
    tiles = seq // tm
    halos_per_tile = tm // HALO
    last_halo = n // HALO - 1
    tok = lambda w: pl.BlockSpec((tm, w), lambda i: (i, 0))
    prev = pl.BlockSpec((HALO, D_MODEL), lambda i: (jnp.maximum(i * halos_per_tile - 1, 0), 0))
    nxt = pl.BlockSpec((HALO, D_MODEL), lambda i: (jnp.minimum((i + 1) * halos_per_tile, last_halo), 0))
    tab = pl.BlockSpec((tm, LANES), lambda i: (i % tiles, 0))
    widths = (QKV_A, Q_B, K_B, V_B, LANES)
    out_specs = [tok(w) for w in widths]
    out_shape = [jax.ShapeDtypeStruct((n, w), F32) for w in widths]
    if not latent:
        assert tiles == 1
        out_specs += [pl.BlockSpec((1, K_B, tm), lambda i: (i, 0, 0))] * 2
        out_shape += [jax.ShapeDtypeStruct((n // tm, K_B, tm), F32)] * 2
    return pl.pallas_call(
        functools.partial(_inproj_kernel, latent=latent, tiles=tiles),
        grid=(n // tm,),
        in_specs=[tok(D_MODEL), prev, nxt,
                  _mod_spec(3, tm, tokens_per_row, row0),
                  _mod_spec(4, tm, tokens_per_row, row0),
                  _const_spec((1, D_MODEL)),
                  _const_spec((D_MODEL, IN_WIDTH_P)),
                  tab, tab,
                  _const_spec((CONV_W, QKV_A)),
                  _const_spec((1, LANES)),
                  _const_spec((1, LANES))],
        out_specs=out_specs,
        out_shape=out_shape,
        compiler_params=pltpu.CompilerParams(dimension_semantics=("parallel",),
                                             vmem_limit_bytes=VMEM_BIG),
        name="inproj_latent" if latent else "inproj_context",
    )(x, x, x, mod4, mod4, norm_w, w_mix, cos_t, sin_t, conv_w, alog_p, dtb_p)


TRI_BASE = 16
SCAN_CHUNKS_PER_STEP = 8


def _unit_tri_inverses(lows):
    n = lows[0].shape[0]
    row = lax.broadcasted_iota(jnp.int32, (n, n), 0)
    col = lax.broadcasted_iota(jnp.int32, (n, n), 1)
    base = row // TRI_BASE == col // TRI_BASE
    eye = jnp.where(row == col, 1.0, 0.0)
    ps = [-jnp.where(base, low, 0.0) for low in lows]
    ts = [eye + p for p in ps]
    ps = [_dot(p, p) for p in ps]
    for _ in range(int(np.log2(TRI_BASE)) - 2):
        both = [_dot(jnp.concatenate([p.astype(BF), t.astype(BF)], axis=0), p) for p, t in zip(ps, ts)]
        ts = [t + b[n:] for t, b in zip(ts, both)]
        ps = [b[:n] for b in both]
    ts = [t + _dot(t, p) for t, p in zip(ts, ps)]
    size = TRI_BASE
    while size < n:
        pair = (row // (2 * size) == col // (2 * size)) & (row // size != col // size)
        mids = [_dot(jnp.where(pair, low, 0.0), t) for low, t in zip(lows, ts)]
        ts = [t - _dot(t, m) for t, m in zip(ts, mids)]
        size *= 2
    return ts


def _scan_kernel(*refs, steps, chunks, has_init, emit_state):
    xf_ref, xb_ref, gf_ref, gb_ref = refs[:4]
    rest = list(refs[4:])
    s0_refs = (rest.pop(0), rest.pop(0)) if has_init else None
    of_ref, ob_ref = rest.pop(0), rest.pop(0)
    out_state_refs = (rest.pop(0), rest.pop(0)) if emit_state else None
    state, = rest
    c = pl.program_id(1)

    @pl.when(c == 0)
    def _():
        if has_init:
            state[0:H_A] = s0_refs[0][0]
            state[H_A:2 * H_A] = s0_refs[1][0]
        else:
            state[...] = jnp.zeros(state.shape, F32)

    row = lax.broadcasted_iota(jnp.int32, (CHUNK, CHUNK), 0)
    col = lax.broadcasted_iota(jnp.int32, (CHUNK, CHUNK), 1)
    x_refs, g_refs, o_refs = (xf_ref, xb_ref), (gf_ref, gb_ref), (of_ref, ob_ref)
    incl = (row >= col, row <= col)
    strict = (row > col, row < col)
    chains = [(d, hh) for d in range(2) for hh in range(H_A)]
    nch = len(chains)

    def one_chunk(sub, carry):
        rows = (pl.ds(pl.multiple_of(sub * CHUNK, CHUNK), CHUNK),
                pl.ds(pl.multiple_of((chunks - 1 - sub) * CHUNK, CHUNK), CHUNK))
        gbeta = [g_refs[d][rows[d], :] for d in range(2)]
        gc = [jnp.dot(jnp.where(incl[d], 1.0, 0.0), gbeta[d], precision=lax.Precision.HIGHEST,
                      preferred_element_type=F32) for d in range(2)]
        gct = [g.T for g in gc]
        q = [x_refs[d][rows[d], hh * DK_A:(hh + 1) * DK_A] for d, hh in chains]
        k = [x_refs[d][rows[d], (H_A + hh) * DK_A:(H_A + hh + 1) * DK_A] for d, hh in chains]
        v = [x_refs[d][rows[d], (2 * H_A + hh) * DK_A:(2 * H_A + hh + 1) * DK_A] for d, hh in chains]
        gcol = [gc[d][:, d * H_A + hh:d * H_A + hh + 1] for d, hh in chains]
        beta = [gbeta[d][:, DEC_A + d * H_A + hh:DEC_A + d * H_A + hh + 1] for d, hh in chains]
        decay = [jnp.exp(jnp.where(incl[d], gcol[i] - gct[d][d * H_A + hh:d * H_A + hh + 1, :], NEG))
                 for i, (d, hh) in enumerate(chains)]
        qk_kk = [_dot_nt(jnp.concatenate([q[i].astype(BF), k[i].astype(BF)], axis=0), k[i]) for i in range(nch)]
        tmat = _unit_tri_inverses([jnp.where(strict[d], qk_kk[i][CHUNK:] * beta[i] * decay[i], 0.0)
                                   for i, (d, hh) in enumerate(chains)])
        egc = [jnp.exp(g) for g in gcol]
        uw = [_dot(tmat[i], jnp.concatenate([v[i] * beta[i], k[i] * (beta[i] * egc[i])], axis=1))
              for i in range(nch)]
        g_last = [gcol[i][CHUNK - 1:CHUNK] if d == 0 else gcol[i][0:1] for i, (d, hh) in enumerate(chains)]
        s_old = [state[i] for i in range(nch)]
        s_bf = [s.astype(BF) for s in s_old]
        ws_qs = [_dot(jnp.concatenate([uw[i][:, DV_A:].astype(BF), (q[i] * egc[i]).astype(BF)], axis=0), s_bf[i])
                 for i in range(nch)]
        v_new = [uw[i][:, :DV_A] - ws_qs[i][:CHUNK] for i in range(nch)]
        kd_t = [(k[i] * jnp.exp(g_last[i] - gcol[i])).T for i in range(nch)]
        av_kv = [_dot(jnp.concatenate([(qk_kk[i][:CHUNK] * decay[i]).astype(BF), kd_t[i].astype(BF)], axis=0),
                      v_new[i]) for i in range(nch)]
        for i, (d, hh) in enumerate(chains):
            o_refs[d][rows[d], hh * DV_A:(hh + 1) * DV_A] = ws_qs[i][CHUNK:] + av_kv[i][:CHUNK]
        for i in range(nch):
            state[i] = s_old[i] * jnp.exp(g_last[i]) + av_kv[i][CHUNK:]
        return carry

    lax.fori_loop(0, chunks, one_chunk, 0, unroll=True)

    if emit_state:
        @pl.when(c == steps - 1)
        def _():
            out_state_refs[0][0] = state[0:H_A]
            out_state_refs[1][0] = state[H_A:2 * H_A]


def _delta_scan(qkvc, gbeta, init, nb, seq, emit_state):
    n = qkvc.shape[0]
    chunks = min(SCAN_CHUNKS_PER_STEP, seq // CHUNK)
    rows = chunks * CHUNK
    steps = seq // rows
    fwd = lambda w: pl.BlockSpec((rows, w), lambda b, c: (b * steps + c, 0))
    bwd = lambda w: pl.BlockSpec((rows, w), lambda b, c: (b * steps + steps - 1 - c, 0))
    st = pl.BlockSpec((1, H_A, DK_A, DV_A), lambda b, c: (b, 0, 0, 0))
    st_shape = jax.ShapeDtypeStruct((nb, H_A, DK_A, DV_A), F32)
    o_shape = jax.ShapeDtypeStruct((n, Z_A), F32)
    has_init = init is not None
    return pl.pallas_call(
        functools.partial(_scan_kernel, steps=steps, chunks=chunks, has_init=has_init, emit_state=emit_state),
        grid=(nb, steps),
        in_specs=[fwd(QKV_A), bwd(QKV_A), fwd(LANES), bwd(LANES)] + ([st, st] if has_init else []),
        out_specs=[fwd(Z_A), bwd(Z_A)] + ([st, st] if emit_state else []),
        out_shape=[o_shape, o_shape] + ([st_shape, st_shape] if emit_state else []),
        scratch_shapes=[pltpu.VMEM((2 * H_A, DK_A, DV_A), F32)],
        compiler_params=pltpu.CompilerParams(dimension_semantics=("parallel", "arbitrary")),
        name="delta_scan",
    )(qkvc, qkvc, gbeta, gbeta, *(init if has_init else ()))


def _attn_core(q, keys, vals, valid, sink_ref, write):
    nq = q.shape[0]
    group = H_B // KV_B
    klane = lax.broadcasted_iota(jnp.int32, keys.shape, 1)
    krot = pltpu.roll(keys, HD_B, 1)
    qlow = lax.broadcasted_iota(jnp.int32, (nq, LANES), 1) < HD_B
    v_bf = vals.astype(BF)
    kdup = [jnp.where(klane < HD_B, keys, krot).astype(BF), jnp.where(klane < HD_B, krot, keys).astype(BF)]
    scores = []
    for head in range(H_B):
        qt = q[:, (head // 2) * LANES:(head // 2 + 1) * LANES]
        qh = jnp.where(qlow, qt, 0.0) if head % 2 == 0 else jnp.where(qlow, 0.0, qt)
        scores.append(_dot_nt(qh, kdup[head // group]))
    probs, inv_den = [], []
    for head, s in enumerate(scores):
        if valid is not None:
            s = jnp.where(valid, s, NEG)
        snk = sink_ref[head]
        m = jnp.maximum(jnp.max(s, axis=-1, keepdims=True), snk)
        p = jnp.exp(s - m)
        inv_den.append(1.0 / (jnp.sum(p, axis=-1, keepdims=True) + jnp.exp(snk - m)))
        probs.append(p.astype(BF))
    o_all = [jnp.dot(jnp.concatenate(probs[g * group:(g + 1) * group], axis=0), v_bf,
                     preferred_element_type=F32) for g in range(KV_B)]
    for g in range(KV_B):
        for t in range(group // 2):
            a = o_all[g][(2 * t) * nq:(2 * t + 1) * nq] * inv_den[g * group + 2 * t]
            b = o_all[g][(2 * t + 1) * nq:(2 * t + 2) * nq] * inv_den[g * group + 2 * t + 1]
            if g == 0:
                tile = jnp.where(qlow, a, pltpu.roll(b, HD_B, 1))
            else:
                tile = jnp.where(qlow, pltpu.roll(a, HD_B, 1), b)
            write((g * group // 2 + t) * LANES, tile)


def _ctx_attn_kernel(sink_ref, q_ref, k_ref, v_ref, o_ref):
    def write(col, tile):
        o_ref[:, col:col + LANES] = tile

    _attn_core(q_ref[...] * HD_B ** -0.5, k_ref[...], v_ref[...], None, sink_ref, write)


def _context_attention(q, k, v, sink, nb, seq):
    n = q.shape[0]
    blk = lambda w: pl.BlockSpec((seq, w), lambda b: (b, 0))
    return pl.pallas_call(
        _ctx_attn_kernel,
        grid=(nb,),
        in_specs=[pl.BlockSpec(memory_space=pltpu.SMEM), blk(Q_B), blk(K_B), blk(V_B)],
        out_specs=blk(Q_B),
        out_shape=jax.ShapeDtypeStruct((n, Q_B), F32),
        compiler_params=pltpu.CompilerParams(dimension_semantics=("parallel",)),
        name="context_attention",
    )(sink, q, k, v)


WIN_SUB = 8


def _win_attn_kernel(sink_ref, q_ref, k_ref, v_ref, kx_ref, vx_ref, o_ref):
    step = pl.program_id(1)
    seq = k_ref.shape[0]
    n_local = 3 * WINDOW
    nk = n_local + kx_ref.shape[1]
    r = lax.broadcasted_iota(jnp.int32, (WINDOW, nk), 0)
    j = lax.broadcasted_iota(jnp.int32, (WINDOW, nk), 1)

    def block(sub, carry):
        i = step * WIN_SUB + sub
        start = pl.multiple_of(jnp.clip((i - 1) * WINDOW, 0, seq - n_local), WINDOW)
        keys = jnp.concatenate([k_ref[pl.ds(start, n_local), :], kx_ref[0]], axis=0)
        vals = jnp.concatenate([v_ref[pl.ds(start, n_local), :], vx_ref[0]], axis=0)
        dist = (i * WINDOW - start) + r - j
        valid = ((dist <= WINDOW) & (dist >= -WINDOW)) | (j >= n_local)
        rows = pl.ds(pl.multiple_of(sub * WINDOW, WINDOW), WINDOW)

        def write(col, tile):
            o_ref[rows, col:col + LANES] = tile

        _attn_core(q_ref[rows, :] * HD_B ** -0.5, keys, vals, valid, sink_ref, write)
        return carry

    lax.fori_loop(0, WIN_SUB, block, 0, unroll=True)


def _window_attention(q, k, v, k_ctx, v_ctx, sink, nb, seq):
    n = q.shape[0]
    tq = WIN_SUB * WINDOW
    steps = seq // tq
    n_ctx = k_ctx.shape[1]
    qblk = pl.BlockSpec((tq, Q_B), lambda b, i: (b * steps + i, 0))
    kv = pl.BlockSpec((seq, K_B), lambda b, i: (b, 0))
    ctx = pl.BlockSpec((1, n_ctx, K_B), lambda b, i: (b, 0, 0))
    return pl.pallas_call(
        _win_attn_kernel,
        grid=(nb, steps),
        in_specs=[pl.BlockSpec(memory_space=pltpu.SMEM), qblk, kv, kv, ctx, ctx],
        out_specs=qblk,
        out_shape=jax.ShapeDtypeStruct((n, Q_B), F32),
        compiler_params=pltpu.CompilerParams(dimension_semantics=("parallel", "parallel")),
        name="window_attention",
    )(sink, q, k, v, k_ctx, v_ctx)


def _merge_kernel(x_ref, of_ref, ob_ref, yb_ref, sh_ref, sc_ref, g2_ref, nw_ref, on_ref,
                  wzg_ref, woa_ref, wob_ref, wout_ref, o_ref):
    x = x_ref[...]
    h = _modulated_norm(x, nw_ref, sh_ref, sc_ref)
    z = jnp.dot(h, wzg_ref[:, :Z_A], preferred_element_type=F32)
    o = of_ref[...] + ob_ref[...]
    heads = []
    for hh in range(H_A):
        sl = slice(hh * DV_A, (hh + 1) * DV_A)
        heads.append(_rms(o[:, sl], on_ref[...]) * _silu(z[:, sl]))
    y_a = jnp.concatenate(heads, axis=1)
    gate_a = jnp.dot(h, wzg_ref[:, Z_A:Z_A + D_MODEL], preferred_element_type=F32)
    gate_b = jnp.dot(h, wzg_ref[:, Z_A + D_MODEL:], preferred_element_type=F32)
    m = _sigmoid(gate_a) * _dot(y_a, woa_ref[...]) + _sigmoid(gate_b) * _dot(yb_ref[...], wob_ref[...])
    o_ref[...] = x + g2_ref[0, 0] * _dot(m, wout_ref[...])


def _merge(x, o_f, o_b, y_b, mod4, tokens_per_row, row0, norm_w, onorm, w_zg, w_oa, w_ob, w_out):
    n = x.shape[0]
    tm = 512
    tok = lambda w: pl.BlockSpec((tm, w), lambda i: (i, 0))
    return pl.pallas_call(
        _merge_kernel,
        grid=(n // tm,),
        in_specs=[tok(D_MODEL), tok(Z_A), tok(Z_A), tok(Q_B),
                  _mod_spec(3, tm, tokens_per_row, row0),
                  _mod_spec(4, tm, tokens_per_row, row0),
                  _mod_spec(5, tm, tokens_per_row, row0),
                  _const_spec((1, D_MODEL)),
                  _const_spec((1, DV_A)),
                  _const_spec((D_MODEL, Z_A + GATES)),
                  _const_spec((H_A * DV_A, D_MODEL)),
                  _const_spec((H_B * HD_B, D_MODEL)),
                  _const_spec((D_MODEL, D_MODEL))],
        out_specs=tok(D_MODEL),
        out_shape=jax.ShapeDtypeStruct((n, D_MODEL), F32),
        compiler_params=pltpu.CompilerParams(dimension_semantics=("parallel",),
                                             vmem_limit_bytes=VMEM_BIG),
        name="merge",
    )(x, o_f, o_b, y_b, mod4, mod4, mod4, norm_w, onorm, w_zg, w_oa, w_ob, w_out)


def _rope_tables(seq):
    rows = seq // GRID_W
    row = jnp.repeat(jnp.arange(rows, dtype=F32), GRID_W)
    col = jnp.tile(jnp.arange(GRID_W, dtype=F32), rows)
    inv = jnp.power(ROPE_BASE, -jnp.arange(ROPE_AXIS_PAIRS, dtype=F32) / ROPE_AXIS_PAIRS)
    ang = jnp.concatenate([row[:, None] * inv, col[:, None] * inv], axis=-1)
    cos, sin = jnp.cos(ang), jnp.sin(ang)
    cos_t = jnp.tile(cos, (1, LANES // (HD_B // 2)))
    sin_t = jnp.tile(jnp.concatenate([-sin, sin], axis=-1), (1, LANES // HD_B))
    return cos_t, sin_t


def _split_w_in(w_in):
    edges = np.cumsum((0, QKV_A, Z_A, DEC_A, BETA_A, Q_B, K_B, V_B, GATES))
    part = lambda i: w_in[:, int(edges[i]):int(edges[i + 1])].astype(BF)
    pad = jnp.zeros((D_MODEL, LANES - DEC_A - BETA_A), BF)
    w_mix = jnp.concatenate([part(0), part(4), part(5), part(6), part(2), part(3), pad], axis=1)
    w_zg = jnp.concatenate([part(1), part(7)], axis=1)
    return w_mix, w_zg


def _cache_layout(t, nb, seq):
    return t.reshape(nb, KV_B, HD_B, seq).transpose(0, 3, 1, 2)[:, None]


def kernel(x_prompt, x_sample, state_delta_fwd, state_delta_bwd, cache_k, cache_v, c, c_ctx, ada_w, ada_b, norm_ffn1, ffn1_w13, ffn1_w2, norm_mix, w_in, conv_w, a_log, dt_bias, onorm_a, w_oa, w_ob, w_out, sink, norm_ffn2, ffn2_w13, ffn2_w2, norm_final):
    bp, sp = x_prompt.shape[:2]
    bs, ts = x_sample.shape[:2]
    layer = 0

    mod_rows = 16
    cvec = jnp.concatenate([c_ctx[None, :], c, jnp.zeros((mod_rows - 1 - bs, D_MODEL), F32)], axis=0)
    mod4 = _modulation(cvec, ada_w[layer], ada_b[layer]).reshape(mod_rows, N_MOD, 1, D_MODEL)

    row = lambda a: a.reshape(1, -1)
    lane_pad = lambda a: jnp.pad(a.reshape(1, -1), ((0, 0), (0, LANES - a.size)))
    w13_1, w2_1 = ffn1_w13[layer].astype(BF), ffn1_w2[layer].astype(BF)
    w13_2, w2_2 = ffn2_w13[layer].astype(BF), ffn2_w2[layer].astype(BF)
    w_mix, w_zg = _split_w_in(w_in[layer])
    w_oa_b, w_ob_b, w_out_b = w_oa[layer].astype(BF), w_ob[layer].astype(BF), w_out[layer].astype(BF)
    alog_p, dtb_p = lane_pad(a_log[layer]), lane_pad(dt_bias[layer])
    cos_t, sin_t = _rope_tables(ts)
    final_w = row(norm_final)

    def run(x, nb, seq, tokens_per_row, row0, init, ctx):
        latent = ctx is not None
        x1 = _ffn(x, mod4, (0, 1, 2), tokens_per_row, row0, row(norm_ffn1[layer]), w13_1, w2_1,
                  final_w, final=False)
        mix = _inproj(x1, mod4, tokens_per_row, row0, row(norm_mix[layer]), w_mix, cos_t, sin_t,
                      conv_w[layer], alog_p, dtb_p, seq, latent)
        qkvc, qb, kb, vb, gbeta = mix[:5]
        scan = _delta_scan(qkvc, gbeta, init, nb, seq, emit_state=not latent)
        if latent:
            y_b = _window_attention(qb, kb, vb, ctx[0], ctx[1], sink[layer], nb, seq)
        else:
            y_b = _context_attention(qb, kb, vb, sink[layer], nb, seq)
        x2 = _merge(x1, scan[0], scan[1], y_b, mod4, tokens_per_row, row0, row(norm_mix[layer]),
                    row(onorm_a[layer]), w_zg, w_oa_b, w_ob_b, w_out_b)
        y = _ffn(x2, mod4, (6, 7, 8), tokens_per_row, row0, row(norm_ffn2[layer]), w13_2, w2_2,
                 final_w, final=True)
        return y, scan[2:], mix[5:]

    yp, (s_f, s_b), (k_t, v_t) = run(x_prompt.reshape(bp * sp, D_MODEL), bp, sp, bp * sp, 0, None, None)
    ctx = (cache_k[:, layer].reshape(bs, -1, K_B), cache_v[:, layer].reshape(bs, -1, V_B))
    ys, _, _ = run(x_sample.reshape(bs * ts, D_MODEL), bs, ts, ts, 1,
                   (state_delta_fwd[:, layer], state_delta_bwd[:, layer]), ctx)

    return (yp.reshape(bp, sp, D_MODEL), ys.reshape(bs, ts, D_MODEL),
            s_f[:, None], s_b[:, None], _cache_layout(k_t, bp, sp), _cache_layout(v_t, bp, sp))
```

```python
import functools

import jax
import jax.numpy as jnp
import numpy as np
from jax import lax
from jax.experimental import pallas as pl
from jax.experimental.pallas import tpu as pltpu

D_MODEL = 1024
GRID_W = 64
H_A = 4
DK_A = 128
DV_A = 128
CONV_W = 5
H_B = 8
KV_B = 2
HD_B = 64
WINDOW = 128
ROPE_BASE = 10000.0
ROPE_AXIS_PAIRS = HD_B // 4
D_FF = 2816
HALF_STEP = 0.5
N_MOD = 9
EPS = 1e-6

QKV_A = 2 * H_A * DK_A + H_A * DV_A
Z_A = H_A * DV_A
DEC_A = 2 * H_A
BETA_A = 2 * H_A
Q_B = H_B * HD_B
K_B = KV_B * HD_B
V_B = KV_B * HD_B
GATES = 2 * D_MODEL

LANES = 128
SUBLANES = 8
V7X_VMEM_BYTES = 64 * 1024 * 1024
NEG = -1e30

BF = jnp.bfloat16
F32 = jnp.float32

FFN_TILE = 1024
FFN_CHUNK = 256
IN_TILE_MAX = 512
MERGE_TILE = 512
MOD_COLS = N_MOD * D_MODEL // 8
MOD_ROWS = 16
CHUNK = 128
SCAN_CHUNKS_PER_STEP = 4
WIN_SUB = 8
VMEM_BIG = V7X_VMEM_BYTES // 8 * 7


def _dot(a, b):
    return jnp.dot(a.astype(BF), b.astype(BF), preferred_element_type=F32)


def _dot_nt(a, b):
    return lax.dot_general(a.astype(BF), b.astype(BF), (((1,), (1,)), ((), ())),
                           preferred_element_type=F32)


def _sigmoid(x):
    return 1.0 / (1.0 + jnp.exp(-x))


def _silu(x):
    return x * _sigmoid(x)


def _rms(x, w):
    return x * lax.rsqrt(jnp.mean(x * x, axis=-1, keepdims=True) + EPS) * w


def _modulated_norm(x, nw_ref, sh_ref, sc_ref):
    gain = nw_ref[...] * (1.0 + sc_ref[0, 0])
    xn = x * lax.rsqrt(jnp.mean(x * x, axis=-1, keepdims=True) + EPS)
    return (xn * gain + sh_ref[0, 0]).astype(BF)


def _const_spec(shape):
    nd = len(shape)
    return pl.BlockSpec(shape, lambda *_: (0,) * nd, pipeline_mode=pl.Buffered(1))


def _mod_spec(kind, tm, tokens_per_row, row0):
    return pl.BlockSpec((1, 1, 1, D_MODEL),
                        lambda i: (row0 + (i * tm) // tokens_per_row, kind, 0, 0))


def _mod_kernel(c_ref, w_ref, b_ref, o_ref):
    o_ref[...] = _dot(_silu(c_ref[...]), w_ref[...]) + b_ref[...]


def _modulation(cvec, ada_w, ada_b):
    rows = cvec.shape[0]
    n = N_MOD * D_MODEL
    bn = MOD_COLS
    return pl.pallas_call(
        _mod_kernel,
        grid=(n // bn,),
        in_specs=[pl.BlockSpec((rows, D_MODEL), lambda j: (0, 0)),
                  pl.BlockSpec((D_MODEL, bn), lambda j: (0, j)),
                  pl.BlockSpec((1, bn), lambda j: (0, j))],
        out_specs=pl.BlockSpec((rows, bn), lambda j: (0, j)),
        out_shape=jax.ShapeDtypeStruct((rows, n), F32),
        name="modulation",
    )(cvec, ada_w, ada_b.reshape(1, n))


def _ffn_kernel(x_ref, sh_ref, sc_ref, g_ref, nw_ref, w13_ref, w2_ref, fw_ref, o_ref, *, ff_chunk, final):
    x = x_ref[...]
    h = _modulated_norm(x, nw_ref, sh_ref, sc_ref)
    acc = jnp.zeros(x.shape, F32)
    for j in range(D_FF // ff_chunk):
        a = jnp.dot(h, w13_ref[:, j * ff_chunk:(j + 1) * ff_chunk], preferred_element_type=F32)
        b = jnp.dot(h, w13_ref[:, D_FF + j * ff_chunk:D_FF + (j + 1) * ff_chunk],
                    preferred_element_type=F32)
        t = (_silu(a) * b).astype(BF)
        acc = acc + jnp.dot(t, w2_ref[j * ff_chunk:(j + 1) * ff_chunk, :], preferred_element_type=F32)
    y = x + (HALF_STEP * g_ref[0, 0]) * acc
    if final:
        y = _rms(y, fw_ref[...])
    o_ref[...] = y


def _ffn(x, mod4, kinds, tokens_per_row, row0, norm_w, w13, w2, final_w, final):
    n = x.shape[0]
    tm = FFN_TILE
    tok = pl.BlockSpec((tm, D_MODEL), lambda i: (i, 0))
    return pl.pallas_call(
        functools.partial(_ffn_kernel, ff_chunk=FFN_CHUNK, final=final),
        grid=(n // tm,),
        in_specs=[tok,
                  _mod_spec(kinds[0], tm, tokens_per_row, row0),
                  _mod_spec(kinds[1], tm, tokens_per_row, row0),
                  _mod_spec(kinds[2], tm, tokens_per_row, row0),
                  _const_spec((1, D_MODEL)),
                  _const_spec((D_MODEL, 2 * D_FF)),
                  _const_spec((D_FF, D_MODEL)),
                  _const_spec((1, D_MODEL))],
        out_specs=tok,
        out_shape=jax.ShapeDtypeStruct((n, D_MODEL), F32),
        compiler_params=pltpu.CompilerParams(dimension_semantics=("parallel",),
                                             vmem_limit_bytes=VMEM_BIG),
        name="ffn_final" if final else "ffn",
    )(x, mod4, mod4, mod4, norm_w, w13, w2, final_w)


IN_WIDTH_P = QKV_A + Q_B + K_B + V_B + LANES
HALO = SUBLANES


def _swap_halves(x):
    w = x.shape[-1]
    lane = lax.broadcasted_iota(jnp.int32, x.shape, 1)
    lower = (lane % HD_B) < (HD_B // 2)
    return jnp.where(lower, pltpu.roll(x, w - HD_B // 2, 1), pltpu.roll(x, HD_B // 2, 1))


def _rope(p, cos_ref, sin_ref):
    reps = p.shape[1] // LANES
    c = jnp.concatenate([cos_ref[...]] * reps, axis=1) if reps > 1 else cos_ref[...]
    s = jnp.concatenate([sin_ref[...]] * reps, axis=1) if reps > 1 else sin_ref[...]
    return p * c + _swap_halves(p) * s


def _inproj_kernel(xm_ref, xp_ref, xn_ref, sh_ref, sc_ref, nw_ref, w_ref, cos_ref, sin_ref,
                   cw_ref, alog_ref, dtb_ref, qkv_ref, qb_ref, kb_ref, vb_ref, gbeta_ref, *rest,
                   latent, tiles):
    t = pl.program_id(0) % tiles
    x = jnp.concatenate([xp_ref[...], xm_ref[...], xn_ref[...]], axis=0)
    h = _modulated_norm(x, nw_ref, sh_ref, sc_ref)
    rows = x.shape[0]
    tile = rows - 2 * HALO
    blocks = rows // SUBLANES
    hm = h[HALO:HALO + tile]

    def project(first, width, lhs=hm):
        return jnp.dot(lhs, w_ref[:, first:first + width], preferred_element_type=F32)

    def conv_section(raw, first):
        width = raw.shape[1]
        r3 = raw.reshape(blocks, SUBLANES, width)
        r3 = jnp.concatenate([jnp.where(t == 0, 0.0, r3[0:1]), r3[1:blocks - 1],
                              jnp.where(t == tiles - 1, 0.0, r3[blocks - 1:blocks])], axis=0)
        sub = lax.broadcasted_iota(jnp.int32, (1, SUBLANES, width), 1)
        acc = None
        for j in range(CONV_W):
            s = CONV_W // 2 - j
            tap = cw_ref[j:j + 1, first:first + width].reshape(1, 1, width)
            if s == 0:
                sh = r3[1:blocks - 1]
            else:
                if s > 0:
                    m = jnp.where(sub < SUBLANES - s, r3[1:blocks - 1], r3[0:blocks - 2])
                else:
                    m = jnp.where(sub >= -s, r3[1:blocks - 1], r3[2:blocks])
                sh = pltpu.roll(m, s % SUBLANES, 1)
            acc = sh * tap if acc is None else acc + sh * tap
        y = _silu(acc.reshape(tile, width))
        for hh in range(width // DK_A):
            v = y[:, hh * DK_A:(hh + 1) * DK_A]
            if first < 2 * H_A * DK_A:
                scale = DK_A ** -0.5 if first < H_A * DK_A else 1.0
                v = v * (lax.rsqrt(jnp.sum(v * v, axis=-1, keepdims=True) + EPS) * scale)
            qkv_ref[:, first + hh * DK_A:first + (hh + 1) * DK_A] = v

    sec = H_A * DK_A
    qb0 = QKV_A
    kv0 = qb0 + Q_B
    db0 = kv0 + K_B + V_B
    for i in range(QKV_A // sec):
        conv_section(project(i * sec, sec, h), i * sec)
    qb = project(qb0, Q_B)
    qb_ref[...] = _rope(qb, cos_ref, sin_ref) if latent else qb
    kv = project(kv0, K_B + V_B)
    kb_ref[...] = _rope(kv[:, :K_B], cos_ref, sin_ref) if latent else kv[:, :K_B]
    vb_ref[...] = kv[:, K_B:]
    if not latent:
        kt_ref, vt_ref = rest
        kt_ref[0] = kv[:, :K_B].T
        vt_ref[0] = kv[:, K_B:].T
    db = project(db0, LANES)

    xg = db + dtb_ref[...]
    softplus = jnp.maximum(xg, 0.0) + jnp.log1p(jnp.exp(-jnp.abs(xg)))
    g = -jnp.exp(alog_ref[...]) * softplus
    lane = lax.broadcasted_iota(jnp.int32, db.shape, 1)
    gbeta_ref[...] = jnp.where(lane < DEC_A, g, jnp.where(lane < DEC_A + BETA_A, _sigmoid(db), 0.0))


def _inproj(x, mod4, tokens_per_row, row0, norm_w, w_mix, cos_t, sin_t, conv_w, alog_p, dtb_p, seq, latent):
    n = x.shape[0]
    tm = min(IN_TILE_MAX, seq)
    tiles = seq // tm
    halos_per_tile = tm // HALO
    last_halo = n // HALO - 1
    tok = lambda w: pl.BlockSpec((tm, w), lambda i: (i, 0))
    prev = pl.BlockSpec((HALO, D_MODEL), lambda i: (jnp.maximum(i * halos_per_tile - 1, 0), 0))
    nxt = pl.BlockSpec((HALO, D_MODEL), lambda i: (jnp.minimum((i + 1) * halos_per_tile, last_halo), 0))
    tab = pl.BlockSpec((tm, LANES), lambda i: (i % tiles, 0))
    widths = (QKV_A, Q_B, K_B, V_B, LANES)
    out_specs = [tok(w) for w in widths]
    out_shape = [jax.ShapeDtypeStruct((n, w), F32) for w in widths]
    if not latent:
        assert tiles == 1
        out_specs += [pl.BlockSpec((1, K_B, tm), lambda i: (i, 0, 0))] * 2
        out_shape += [jax.ShapeDtypeStruct((n // tm, K_B, tm), F32)] * 2
    return pl.pallas_call(
        functools.partial(_inproj_kernel, latent=latent, tiles=tiles),
        grid=(n // tm,),
        in_specs=[tok(D_MODEL), prev, nxt,
                  _mod_spec(3, tm, tokens_per_row, row0),
                  _mod_spec(4, tm, tokens_per_row, row0),
                  _const_spec((1, D_MODEL)),
                  _const_spec((D_MODEL, IN_WIDTH_P)),
                  tab, tab,
                  _const_spec((CONV_W, QKV_A)),
                  _const_spec((1, LANES)),
                  _const_spec((1, LANES))],
        out_specs=out_specs,
        out_shape=out_shape,
        compiler_params=pltpu.CompilerParams(dimension_semantics=("parallel",),
                                             vmem_limit_bytes=VMEM_BIG),
        name="inproj_latent" if latent else "inproj_context",
    )(x, x, x, mod4, mod4, norm_w, w_mix, cos_t, sin_t, conv_w, alog_p, dtb_p)


TRI_BASE = 16


def _unit_tri_inverses(lows):
    n = lows[0].shape[0]
    row = lax.broadcasted_iota(jnp.int32, (n, n), 0)
    col = lax.broadcasted_iota(jnp.int32, (n, n), 1)
    base = row // TRI_BASE == col // TRI_BASE
    eye = jnp.where(row == col, 1.0, 0.0)
    ps = [-jnp.where(base, low, 0.0) for low in lows]
    ts = [eye + p for p in ps]
    ps = [_dot(p, p) for p in ps]
    for _ in range(int(np.log2(TRI_BASE)) - 2):
        both = [_dot(jnp.concatenate([p.astype(BF), t.astype(BF)], axis=0), p) for p, t in zip(ps, ts)]
        ts = [t + b[n:] for t, b in zip(ts, both)]
        ps = [b[:n] for b in both]
    ts = [t + _dot(t, p) for t, p in zip(ts, ps)]
    size = TRI_BASE
    while size < n:
        pair = (row // (2 * size) == col // (2 * size)) & (row // size != col // size)
        mids = [_dot(jnp.where(pair, low, 0.0), t) for low, t in zip(lows, ts)]
        ts = [t - _dot(t, m) for t, m in zip(ts, mids)]
        size *= 2
    return ts


def _bf16_terms(x):
    hi = x.astype(BF)
    rest = x - hi.astype(F32)
    mid = rest.astype(BF)
    lo = (rest - mid.astype(F32)).astype(BF)
    return hi, mid, lo


def _scan_kernel(*refs, steps, chunks, has_init, emit_state):
    xf_ref, xb_ref, gf_ref, gb_ref = refs[:4]
    rest = list(refs[4:])
    s0_refs = (rest.pop(0), rest.pop(0)) if has_init else None
    of_ref, ob_ref = rest.pop(0), rest.pop(0)
    out_state_refs = (rest.pop(0), rest.pop(0)) if emit_state else None
    state, = rest
    c = pl.program_id(1)

    @pl.when(c == 0)
    def _():
        if has_init:
            state[0:H_A] = s0_refs[0][0]
            state[H_A:2 * H_A] = s0_refs[1][0]
        else:
            state[...] = jnp.zeros(state.shape, F32)

    row = lax.broadcasted_iota(jnp.int32, (CHUNK, CHUNK), 0)
    col = lax.broadcasted_iota(jnp.int32, (CHUNK, CHUNK), 1)
    x_refs, g_refs, o_refs = (xf_ref, xb_ref), (gf_ref, gb_ref), (of_ref, ob_ref)
    incl = (row >= col, row <= col)
    strict = (row > col, row < col)
    tri = [jnp.where(m, 1.0, 0.0).astype(BF) for m in incl]
    chains = [(d, hh) for d in range(2) for hh in range(H_A)]
    nch = len(chains)

    def one_chunk(sub, carry):
        rows = (pl.ds(pl.multiple_of(sub * CHUNK, CHUNK), CHUNK),
                pl.ds(pl.multiple_of((chunks - 1 - sub) * CHUNK, CHUNK), CHUNK))
        gbeta = [g_refs[d][rows[d], :] for d in range(2)]
        gc = [sum(jnp.dot(tri[d], part, preferred_element_type=F32) for part in _bf16_terms(gbeta[d]))
              for d in range(2)]
        gct = [g.T for g in gc]
        q = [x_refs[d][rows[d], hh * DK_A:(hh + 1) * DK_A] for d, hh in chains]
        k = [x_refs[d][rows[d], (H_A + hh) * DK_A:(H_A + hh + 1) * DK_A] for d, hh in chains]
        v = [x_refs[d][rows[d], (2 * H_A + hh) * DK_A:(2 * H_A + hh + 1) * DK_A] for d, hh in chains]
        gcol = [gc[d][:, d * H_A + hh:d * H_A + hh + 1] for d, hh in chains]
        beta = [gbeta[d][:, DEC_A + d * H_A + hh:DEC_A + d * H_A + hh + 1] for d, hh in chains]
        decay = [jnp.exp(jnp.where(incl[d], gcol[i] - gct[d][d * H_A + hh:d * H_A + hh + 1, :], NEG))
                 for i, (d, hh) in enumerate(chains)]
        qk_kk = [_dot_nt(jnp.concatenate([q[i].astype(BF), k[i].astype(BF)], axis=0), k[i]) for i in range(nch)]
        tmat = _unit_tri_inverses([jnp.where(strict[d], qk_kk[i][CHUNK:] * beta[i] * decay[i], 0.0)
                                   for i, (d, hh) in enumerate(chains)])
        egc = [jnp.exp(g) for g in gcol]
        uw = [_dot(tmat[i], jnp.concatenate([v[i] * beta[i], k[i] * (beta[i] * egc[i])], axis=1))
              for i in range(nch)]
        g_last = [gcol[i][CHUNK - 1:CHUNK] if d == 0 else gcol[i][0:1] for i, (d, hh) in enumerate(chains)]
        s_old = [state[i] for i in range(nch)]
        s_bf = [s.astype(BF) for s in s_old]
        ws_qs = [_dot(jnp.concatenate([uw[i][:, DV_A:].astype(BF), (q[i] * egc[i]).astype(BF)], axis=0), s_bf[i])
                 for i in range(nch)]
        v_new = [uw[i][:, :DV_A] - ws_qs[i][:CHUNK] for i in range(nch)]
        kd_t = [(k[i] * jnp.exp(g_last[i] - gcol[i])).T for i in range(nch)]
        av_kv = [_dot(jnp.concatenate([(qk_kk[i][:CHUNK] * decay[i]).astype(BF), kd_t[i].astype(BF)], axis=0),
                      v_new[i]) for i in range(nch)]
        for i, (d, hh) in enumerate(chains):
            o_refs[d][rows[d], hh * DV_A:(hh + 1) * DV_A] = ws_qs[i][CHUNK:] + av_kv[i][:CHUNK]
        for i in range(nch):
            state[i] = s_old[i] * jnp.exp(g_last[i]) + av_kv[i][CHUNK:]
        return carry

    lax.fori_loop(0, chunks, one_chunk, 0, unroll=True)

    if emit_state:
        @pl.when(c == steps - 1)
        def _():
            out_state_refs[0][0] = state[0:H_A]
            out_state_refs[1][0] = state[H_A:2 * H_A]


def _delta_scan(qkvc, gbeta, init, nb, seq, emit_state):
    n = qkvc.shape[0]
    chunks = min(SCAN_CHUNKS_PER_STEP, seq // CHUNK)
    rows = chunks * CHUNK
    steps = seq // rows
    fwd = lambda w: pl.BlockSpec((rows, w), lambda b, c: (b * steps + c, 0))
    bwd = lambda w: pl.BlockSpec((rows, w), lambda b, c: (b * steps + steps - 1 - c, 0))
    st = pl.BlockSpec((1, H_A, DK_A, DV_A), lambda b, c: (b, 0, 0, 0))
    st_shape = jax.ShapeDtypeStruct((nb, H_A, DK_A, DV_A), F32)
    o_shape = jax.ShapeDtypeStruct((n, Z_A), F32)
    has_init = init is not None
    return pl.pallas_call(
        functools.partial(_scan_kernel, steps=steps, chunks=chunks, has_init=has_init, emit_state=emit_state),
        grid=(nb, steps),
        in_specs=[fwd(QKV_A), bwd(QKV_A), fwd(LANES), bwd(LANES)] + ([st, st] if has_init else []),
        out_specs=[fwd(Z_A), bwd(Z_A)] + ([st, st] if emit_state else []),
        out_shape=[o_shape, o_shape] + ([st_shape, st_shape] if emit_state else []),
        scratch_shapes=[pltpu.VMEM((2 * H_A, DK_A, DV_A), F32)],
        compiler_params=pltpu.CompilerParams(dimension_semantics=("parallel", "arbitrary")),
        name="delta_scan",
    )(qkvc, qkvc, gbeta, gbeta, *(init if has_init else ()))


def _attn_core(q, keys, vals, valid, sink_ref, write):
    nq = q.shape[0]
    group = H_B // KV_B
    klane = lax.broadcasted_iota(jnp.int32, keys.shape, 1)
    krot = pltpu.roll(keys, HD_B, 1)
    qlow = lax.broadcasted_iota(jnp.int32, (nq, LANES), 1) < HD_B
    v_bf = vals.astype(BF)
    kdup = [jnp.where(klane < HD_B, keys, krot).astype(BF), jnp.where(klane < HD_B, krot, keys).astype(BF)]
    scores = []
    for head in range(H_B):
        qt = q[:, (head // 2) * LANES:(head // 2 + 1) * LANES]
        qh = jnp.where(qlow, qt, 0.0) if head % 2 == 0 else jnp.where(qlow, 0.0, qt)
        scores.append(_dot_nt(qh, kdup[head // group]))
    probs, inv_den = [], []
    for head, s in enumerate(scores):
        if valid is not None:
            s = jnp.where(valid, s, NEG)
        snk = sink_ref[head]
        m = jnp.maximum(jnp.max(s, axis=-1, keepdims=True), snk)
        p = jnp.exp(s - m)
        inv_den.append(1.0 / (jnp.sum(p, axis=-1, keepdims=True) + jnp.exp(snk - m)))
        probs.append(p.astype(BF))
    o_all = [jnp.dot(jnp.concatenate(probs[g * group:(g + 1) * group], axis=0), v_bf,
                     preferred_element_type=F32) for g in range(KV_B)]
    for g in range(KV_B):
        for t in range(group // 2):
            a = o_all[g][(2 * t) * nq:(2 * t + 1) * nq] * inv_den[g * group + 2 * t]
            b = o_all[g][(2 * t + 1) * nq:(2 * t + 2) * nq] * inv_den[g * group + 2 * t + 1]
            if g == 0:
                tile = jnp.where(qlow, a, pltpu.roll(b, HD_B, 1))
            else:
                tile = jnp.where(qlow, pltpu.roll(a, HD_B, 1), b)
            write((g * group // 2 + t) * LANES, tile)


def _ctx_attn_kernel(sink_ref, q_ref, k_ref, v_ref, o_ref):
    def write(col, tile):
        o_ref[:, col:col + LANES] = tile

    _attn_core(q_ref[...] * HD_B ** -0.5, k_ref[...], v_ref[...], None, sink_ref, write)


def _context_attention(q, k, v, sink, nb, seq):
    n = q.shape[0]
    blk = lambda w: pl.BlockSpec((seq, w), lambda b: (b, 0))
    return pl.pallas_call(
        _ctx_attn_kernel,
        grid=(nb,),
        in_specs=[pl.BlockSpec(memory_space=pltpu.SMEM), blk(Q_B), blk(K_B), blk(V_B)],
        out_specs=blk(Q_B),
        out_shape=jax.ShapeDtypeStruct((n, Q_B), F32),
        compiler_params=pltpu.CompilerParams(dimension_semantics=("parallel",)),
        name="context_attention",
    )(sink, q, k, v)


def _win_attn_kernel(sink_ref, q_ref, k_ref, v_ref, kx_ref, vx_ref, o_ref):
    step = pl.program_id(1)
    seq = k_ref.shape[0]
    n_local = 3 * WINDOW
    nk = n_local + kx_ref.shape[1]
    r = lax.broadcasted_iota(jnp.int32, (WINDOW, nk), 0)
    j = lax.broadcasted_iota(jnp.int32, (WINDOW, nk), 1)

    def block(sub, carry):
        i = step * WIN_SUB + sub
        start = pl.multiple_of(jnp.clip((i - 1) * WINDOW, 0, seq - n_local), WINDOW)
        keys = jnp.concatenate([k_ref[pl.ds(start, n_local), :], kx_ref[0]], axis=0)
        vals = jnp.concatenate([v_ref[pl.ds(start, n_local), :], vx_ref[0]], axis=0)
        dist = (i * WINDOW - start) + r - j
        valid = ((dist <= WINDOW) & (dist >= -WINDOW)) | (j >= n_local)
        rows = pl.ds(pl.multiple_of(sub * WINDOW, WINDOW), WINDOW)

        def write(col, tile):
            o_ref[rows, col:col + LANES] = tile

        _attn_core(q_ref[rows, :] * HD_B ** -0.5, keys, vals, valid, sink_ref, write)
        return carry

    lax.fori_loop(0, WIN_SUB, block, 0, unroll=True)


def _window_attention(q, k, v, k_ctx, v_ctx, sink, nb, seq):
    n = q.shape[0]
    tq = WIN_SUB * WINDOW
    steps = seq // tq
    n_ctx = k_ctx.shape[1]
    qblk = pl.BlockSpec((tq, Q_B), lambda b, i: (b * steps + i, 0))
    kv = pl.BlockSpec((seq, K_B), lambda b, i: (b, 0))
    ctx = pl.BlockSpec((1, n_ctx, K_B), lambda b, i: (b, 0, 0))
    return pl.pallas_call(
        _win_attn_kernel,
        grid=(nb, steps),
        in_specs=[pl.BlockSpec(memory_space=pltpu.SMEM), qblk, kv, kv, ctx, ctx],
        out_specs=qblk,
        out_shape=jax.ShapeDtypeStruct((n, Q_B), F32),
        compiler_params=pltpu.CompilerParams(dimension_semantics=("parallel", "parallel")),
        name="window_attention",
    )(sink, q, k, v, k_ctx, v_ctx)


def _merge_kernel(x_ref, of_ref, ob_ref, yb_ref, sh_ref, sc_ref, g2_ref, nw_ref, on_ref,
                  wzg_ref, woa_ref, wob_ref, wout_ref, o_ref):
    x = x_ref[...]
    h = _modulated_norm(x, nw_ref, sh_ref, sc_ref)
    z = jnp.dot(h, wzg_ref[:, :Z_A], preferred_element_type=F32)
    o = of_ref[...] + ob_ref[...]
    heads = []
    for hh in range(H_A):
        sl = slice(hh * DV_A, (hh + 1) * DV_A)
        heads.append(_rms(o[:, sl], on_ref[...]) * _silu(z[:, sl]))
    y_a = jnp.concatenate(heads, axis=1)
    gate_a = jnp.dot(h, wzg_ref[:, Z_A:Z_A + D_MODEL], preferred_element_type=F32)
    gate_b = jnp.dot(h, wzg_ref[:, Z_A + D_MODEL:], preferred_element_type=F32)
    m = _sigmoid(gate_a) * _dot(y_a, woa_ref[...]) + _sigmoid(gate_b) * _dot(yb_ref[...], wob_ref[...])
    o_ref[...] = x + g2_ref[0, 0] * _dot(m, wout_ref[...])


def _merge(x, o_f, o_b, y_b, mod4, tokens_per_row, row0, norm_w, onorm, w_zg, w_oa, w_ob, w_out):
    n = x.shape[0]
    tm = MERGE_TILE
    tok = lambda w: pl.BlockSpec((tm, w), lambda i: (i, 0))
    return pl.pallas_call(
        _merge_kernel,
        grid=(n // tm,),
        in_specs=[tok(D_MODEL), tok(Z_A), tok(Z_A), tok(Q_B),
                  _mod_spec(3, tm, tokens_per_row, row0),
                  _mod_spec(4, tm, tokens_per_row, row0),
                  _mod_spec(5, tm, tokens_per_row, row0),
                  _const_spec((1, D_MODEL)),
                  _const_spec((1, DV_A)),
                  _const_spec((D_MODEL, Z_A + GATES)),
                  _const_spec((H_A * DV_A, D_MODEL)),
                  _const_spec((H_B * HD_B, D_MODEL)),
                  _const_spec((D_MODEL, D_MODEL))],
        out_specs=tok(D_MODEL),
        out_shape=jax.ShapeDtypeStruct((n, D_MODEL), F32),
        compiler_params=pltpu.CompilerParams(dimension_semantics=("parallel",),
                                             vmem_limit_bytes=VMEM_BIG),
        name="merge",
    )(x, o_f, o_b, y_b, mod4, mod4, mod4, norm_w, onorm, w_zg, w_oa, w_ob, w_out)


def _rope_tables(seq):
    rows = seq // GRID_W
    row = jnp.repeat(jnp.arange(rows, dtype=F32), GRID_W)
    col = jnp.tile(jnp.arange(GRID_W, dtype=F32), rows)
    inv = jnp.power(ROPE_BASE, -jnp.arange(ROPE_AXIS_PAIRS, dtype=F32) / ROPE_AXIS_PAIRS)
    ang = jnp.concatenate([row[:, None] * inv, col[:, None] * inv], axis=-1)
    cos, sin = jnp.cos(ang), jnp.sin(ang)
    cos_t = jnp.tile(cos, (1, LANES // (HD_B // 2)))
    sin_t = jnp.tile(jnp.concatenate([-sin, sin], axis=-1), (1, LANES // HD_B))
    return cos_t, sin_t


def _split_w_in(w_in):
    edges = np.cumsum((0, QKV_A, Z_A, DEC_A, BETA_A, Q_B, K_B, V_B, GATES))
    part = lambda i: w_in[:, int(edges[i]):int(edges[i + 1])].astype(BF)
    pad = jnp.zeros((D_MODEL, LANES - DEC_A - BETA_A), BF)
    w_mix = jnp.concatenate([part(0), part(4), part(5), part(6), part(2), part(3), pad], axis=1)
    w_zg = jnp.concatenate([part(1), part(7)], axis=1)
    return w_mix, w_zg


def _cache_layout(t, nb, seq):
    return t.reshape(nb, KV_B, HD_B, seq).transpose(0, 3, 1, 2)[:, None]


def kernel(x_prompt, x_sample, state_delta_fwd, state_delta_bwd, cache_k, cache_v, c, c_ctx, ada_w, ada_b, norm_ffn1, ffn1_w13, ffn1_w2, norm_mix, w_in, conv_w, a_log, dt_bias, onorm_a, w_oa, w_ob, w_out, sink, norm_ffn2, ffn2_w13, ffn2_w2, norm_final):
    bp, sp = x_prompt.shape[:2]
    bs, ts = x_sample.shape[:2]
    layer = 0

    cvec = jnp.concatenate([c_ctx[None, :], c, jnp.zeros((MOD_ROWS - 1 - bs, D_MODEL), F32)], axis=0)
    mod4 = _modulation(cvec, ada_w[layer], ada_b[layer]).reshape(MOD_ROWS, N_MOD, 1, D_MODEL)

    row = lambda a: a.reshape(1, -1)
    lane_pad = lambda a: jnp.pad(a.reshape(1, -1), ((0, 0), (0, LANES - a.size)))
    w13_1, w2_1 = ffn1_w13[layer].astype(BF), ffn1_w2[layer].astype(BF)
    w13_2, w2_2 = ffn2_w13[layer].astype(BF), ffn2_w2[layer].astype(BF)
    w_mix, w_zg = _split_w_in(w_in[layer])
    w_oa_b, w_ob_b, w_out_b = w_oa[layer].astype(BF), w_ob[layer].astype(BF), w_out[layer].astype(BF)
    alog_p, dtb_p = lane_pad(a_log[layer]), lane_pad(dt_bias[layer])
    cos_t, sin_t = _rope_tables(ts)
    final_w = row(norm_final)

    def run(x, nb, seq, tokens_per_row, row0, init, ctx):
        latent = ctx is not None
        x1 = _ffn(x, mod4, (0, 1, 2), tokens_per_row, row0, row(norm_ffn1[layer]), w13_1, w2_1,
                  final_w, final=False)
        mix = _inproj(x1, mod4, tokens_per_row, row0, row(norm_mix[layer]), w_mix, cos_t, sin_t,
                      conv_w[layer], alog_p, dtb_p, seq, latent)
        qkvc, qb, kb, vb, gbeta = mix[:5]
        scan = _delta_scan(qkvc, gbeta, init, nb, seq, emit_state=not latent)
        if latent:
            y_b = _window_attention(qb, kb, vb, ctx[0], ctx[1], sink[layer], nb, seq)
        else:
            y_b = _context_attention(qb, kb, vb, sink[layer], nb, seq)
        x2 = _merge(x1, scan[0], scan[1], y_b, mod4, tokens_per_row, row0, row(norm_mix[layer]),
                    row(onorm_a[layer]), w_zg, w_oa_b, w_ob_b, w_out_b)
        y = _ffn(x2, mod4, (6, 7, 8), tokens_per_row, row0, row(norm_ffn2[layer]), w13_2, w2_2,
                 final_w, final=True)
        return y, scan[2:], mix[5:]

    yp, (s_f, s_b), (k_t, v_t) = run(x_prompt.reshape(bp * sp, D_MODEL), bp, sp, bp * sp, 0, None, None)
    ctx = (cache_k[:, layer].reshape(bs, -1, K_B), cache_v[:, layer].reshape(bs, -1, V_B))
    ys, _, _ = run(x_sample.reshape(bs * ts, D_MODEL), bs, ts, ts, 1,
                   (state_delta_fwd[:, layer], state_delta_bwd[:, layer]), ctx)

    return (yp.reshape(bp, sp, D_MODEL), ys.reshape(bs, ts, D_MODEL),
            s_f[:, None], s_b[:, None], _cache_layout(k_t, bp, sp), _cache_layout(v_t, bp, sp))
```

```python
import functools

import jax
import jax.numpy as jnp
import numpy as np
from jax import lax
from jax.experimental import pallas as pl
from jax.experimental.pallas import tpu as pltpu

D_MODEL = 1024
GRID_W = 64
H_A = 4
DK_A = 128
DV_A = 128
CONV_W = 5
H_B = 8
KV_B = 2
HD_B = 64
WINDOW = 128
ROPE_BASE = 10000.0
ROPE_AXIS_PAIRS = HD_B // 4
D_FF = 2816
HALF_STEP = 0.5
N_MOD = 9
EPS = 1e-6

QKV_A = 2 * H_A * DK_A + H_A * DV_A
Z_A = H_A * DV_A
DEC_A = 2 * H_A
BETA_A = 2 * H_A
Q_B = H_B * HD_B
K_B = KV_B * HD_B
V_B = KV_B * HD_B
GATES = 2 * D_MODEL

LANES = 128
SUBLANES = 8
V7X_VMEM_BYTES = 64 * 1024 * 1024
NEG = -1e30

BF = jnp.bfloat16
F32 = jnp.float32

FFN_TILE = 1024
FFN_CHUNK = 256
IN_TILE_MAX = 512
MERGE_TILE = 512
MOD_COLS = N_MOD * D_MODEL // 8
MOD_ROWS = 16
CHUNK = 128
SCAN_CHUNKS_PER_STEP = 4
WIN_SUB = 8
VMEM_BIG = V7X_VMEM_BYTES // 8 * 7


def _dot(a, b):
    return jnp.dot(a.astype(BF), b.astype(BF), preferred_element_type=F32)


def _dot_nt(a, b):
    return lax.dot_general(a.astype(BF), b.astype(BF), (((1,), (1,)), ((), ())),
                           preferred_element_type=F32)


def _sigmoid(x):
    return 0.5 + 0.5 * jnp.tanh(0.5 * x)


def _silu(x):
    half = 0.5 * x
    return half + half * jnp.tanh(half)


def _rms(x, w):
    return x * lax.rsqrt(jnp.mean(x * x, axis=-1, keepdims=True) + EPS) * w


def _modulated_norm(x, nw_ref, sh_ref, sc_ref):
    gain = nw_ref[...] * (1.0 + sc_ref[0, 0])
    xn = x * lax.rsqrt(jnp.mean(x * x, axis=-1, keepdims=True) + EPS)
    return (xn * gain + sh_ref[0, 0]).astype(BF)


def _const_spec(shape):
    nd = len(shape)
    return pl.BlockSpec(shape, lambda *_: (0,) * nd, pipeline_mode=pl.Buffered(1))


def _mod_spec(kind, tm, tokens_per_row, row0):
    return pl.BlockSpec((1, 1, 1, D_MODEL),
                        lambda i: (row0 + (i * tm) // tokens_per_row, kind, 0, 0))


def _mod_kernel(c_ref, w_ref, b_ref, o_ref):
    o_ref[...] = _dot(_silu(c_ref[...]), w_ref[...]) + b_ref[...]


def _modulation(cvec, ada_w, ada_b):
    rows = cvec.shape[0]
    n = N_MOD * D_MODEL
    bn = MOD_COLS
    return pl.pallas_call(
        _mod_kernel,
        grid=(n // bn,),
        in_specs=[pl.BlockSpec((rows, D_MODEL), lambda j: (0, 0)),
                  pl.BlockSpec((D_MODEL, bn), lambda j: (0, j)),
                  pl.BlockSpec((1, bn), lambda j: (0, j))],
        out_specs=pl.BlockSpec((rows, bn), lambda j: (0, j)),
        out_shape=jax.ShapeDtypeStruct((rows, n), F32),
        name="modulation",
    )(cvec, ada_w, ada_b.reshape(1, n))


def _ffn_kernel(x_ref, sh_ref, sc_ref, g_ref, nw_ref, w13_ref, w2_ref, fw_ref, o_ref, *, ff_chunk, final):
    x = x_ref[...]
    h = _modulated_norm(x, nw_ref, sh_ref, sc_ref)
    acc = jnp.zeros(x.shape, F32)
    for j in range(D_FF // ff_chunk):
        a = jnp.dot(h, w13_ref[:, j * ff_chunk:(j + 1) * ff_chunk], preferred_element_type=F32)
        b = jnp.dot(h, w13_ref[:, D_FF + j * ff_chunk:D_FF + (j + 1) * ff_chunk],
                    preferred_element_type=F32)
        t = (_silu(a) * b).astype(BF)
        acc = acc + jnp.dot(t, w2_ref[j * ff_chunk:(j + 1) * ff_chunk, :], preferred_element_type=F32)
    y = x + (HALF_STEP * g_ref[0, 0]) * acc
    if final:
        y = _rms(y, fw_ref[...])
    o_ref[...] = y


def _ffn(x, mod4, kinds, tokens_per_row, row0, norm_w, w13, w2, final_w, final):
    n = x.shape[0]
    tm = FFN_TILE
    tok = pl.BlockSpec((tm, D_MODEL), lambda i: (i, 0))
    return pl.pallas_call(
        functools.partial(_ffn_kernel, ff_chunk=FFN_CHUNK, final=final),
        grid=(n // tm,),
        in_specs=[tok,
                  _mod_spec(kinds[0], tm, tokens_per_row, row0),
                  _mod_spec(kinds[1], tm, tokens_per_row, row0),
                  _mod_spec(kinds[2], tm, tokens_per_row, row0),
                  _const_spec((1, D_MODEL)),
                  _const_spec((D_MODEL, 2 * D_FF)),
                  _const_spec((D_FF, D_MODEL)),
                  _const_spec((1, D_MODEL))],
        out_specs=tok,
        out_shape=jax.ShapeDtypeStruct((n, D_MODEL), F32),
        compiler_params=pltpu.CompilerParams(dimension_semantics=("parallel",),
                                             vmem_limit_bytes=VMEM_BIG),
        name="ffn_final" if final else "ffn",
    )(x, mod4, mod4, mod4, norm_w, w13, w2, final_w)


IN_WIDTH_P = QKV_A + Q_B + K_B + V_B + LANES
HALO = SUBLANES


def _swap_halves(x):
    w = x.shape[-1]
    lane = lax.broadcasted_iota(jnp.int32, x.shape, 1)
    lower = (lane % HD_B) < (HD_B // 2)
    return jnp.where(lower, pltpu.roll(x, w - HD_B // 2, 1), pltpu.roll(x, HD_B // 2, 1))


def _rope(p, cos_ref, sin_ref):
    reps = p.shape[1] // LANES
    c = jnp.concatenate([cos_ref[...]] * reps, axis=1) if reps > 1 else cos_ref[...]
    s = jnp.concatenate([sin_ref[...]] * reps, axis=1) if reps > 1 else sin_ref[...]
    return p * c + _swap_halves(p) * s


def _inproj_kernel(xm_ref, xp_ref, xn_ref, sh_ref, sc_ref, nw_ref, w_ref, cos_ref, sin_ref,
                   cw_ref, alog_ref, dtb_ref, qkv_ref, qb_ref, kb_ref, vb_ref, gbeta_ref, *rest,
                   latent, tiles):
    t = pl.program_id(0) % tiles
    x = jnp.concatenate([xp_ref[...], xm_ref[...], xn_ref[...]], axis=0)
    h = _modulated_norm(x, nw_ref, sh_ref, sc_ref)
    rows = x.shape[0]
    tile = rows - 2 * HALO
    blocks = rows // SUBLANES
    hm = h[HALO:HALO + tile]

    def project(first, width, lhs=hm):
        return jnp.dot(lhs, w_ref[:, first:first + width], preferred_element_type=F32)

    def conv_section(raw, first):
        width = raw.shape[1]
        r3 = raw.reshape(blocks, SUBLANES, width)
        r3 = jnp.concatenate([jnp.where(t == 0, 0.0, r3[0:1]), r3[1:blocks - 1],
                              jnp.where(t == tiles - 1, 0.0, r3[blocks - 1:blocks])], axis=0)
        sub = lax.broadcasted_iota(jnp.int32, (1, SUBLANES, width), 1)
        acc = None
        for j in range(CONV_W):
            s = CONV_W // 2 - j
            tap = cw_ref[j:j + 1, first:first + width].reshape(1, 1, width)
            if s == 0:
                sh = r3[1:blocks - 1]
            else:
                if s > 0:
                    m = jnp.where(sub < SUBLANES - s, r3[1:blocks - 1], r3[0:blocks - 2])
                else:
                    m = jnp.where(sub >= -s, r3[1:blocks - 1], r3[2:blocks])
                sh = pltpu.roll(m, s % SUBLANES, 1)
            acc = sh * tap if acc is None else acc + sh * tap
        y = _silu(acc.reshape(tile, width))
        for hh in range(width // DK_A):
            v = y[:, hh * DK_A:(hh + 1) * DK_A]
            if first < 2 * H_A * DK_A:
                scale = DK_A ** -0.5 if first < H_A * DK_A else 1.0
                v = v * (lax.rsqrt(jnp.sum(v * v, axis=-1, keepdims=True) + EPS) * scale)
            qkv_ref[:, first + hh * DK_A:first + (hh + 1) * DK_A] = v

    sec = H_A * DK_A
    qb0 = QKV_A
    kv0 = qb0 + Q_B
    db0 = kv0 + K_B + V_B
    for i in range(QKV_A // sec):
        conv_section(project(i * sec, sec, h), i * sec)
    qb = project(qb0, Q_B)
    qb_ref[...] = _rope(qb, cos_ref, sin_ref) if latent else qb
    kv = project(kv0, K_B + V_B)
    kb_ref[...] = _rope(kv[:, :K_B], cos_ref, sin_ref) if latent else kv[:, :K_B]
    vb_ref[...] = kv[:, K_B:]
    if not latent:
        kt_ref, vt_ref = rest
        kt_ref[0] = kv[:, :K_B].T
        vt_ref[0] = kv[:, K_B:].T
    db = project(db0, LANES)

    xg = db + dtb_ref[...]
    softplus = jnp.maximum(xg, 0.0) + jnp.log1p(jnp.exp(-jnp.abs(xg)))
    g = -jnp.exp(alog_ref[...]) * softplus
    lane = lax.broadcasted_iota(jnp.int32, db.shape, 1)
    gbeta_ref[...] = jnp.where(lane < DEC_A, g, jnp.where(lane < DEC_A + BETA_A, _sigmoid(db), 0.0))


def _inproj(x, mod4, tokens_per_row, row0, norm_w, w_mix, cos_t, sin_t, conv_w, alog_p, dtb_p, seq, latent):
    n = x.shape[0]
    tm = min(IN_TILE_MAX, seq)
    tiles = seq // tm
    halos_per_tile = tm // HALO
    last_halo = n // HALO - 1
    tok = lambda w: pl.BlockSpec((tm, w), lambda i: (i, 0))
    prev = pl.BlockSpec((HALO, D_MODEL), lambda i: (jnp.maximum(i * halos_per_tile - 1, 0), 0))
    nxt = pl.BlockSpec((HALO, D_MODEL), lambda i: (jnp.minimum((i + 1) * halos_per_tile, last_halo), 0))
    tab = pl.BlockSpec((tm, LANES), lambda i: (i % tiles, 0))
    widths = (QKV_A, Q_B, K_B, V_B, LANES)
    out_specs = [tok(w) for w in widths]
    out_shape = [jax.ShapeDtypeStruct((n, w), F32) for w in widths]
    if not latent:
        assert tiles == 1
        out_specs += [pl.BlockSpec((1, K_B, tm), lambda i: (i, 0, 0))] * 2
        out_shape += [jax.ShapeDtypeStruct((n // tm, K_B, tm), F32)] * 2
    return pl.pallas_call(
        functools.partial(_inproj_kernel, latent=latent, tiles=tiles),
        grid=(n // tm,),
        in_specs=[tok(D_MODEL), prev, nxt,
                  _mod_spec(3, tm, tokens_per_row, row0),
                  _mod_spec(4, tm, tokens_per_row, row0),
                  _const_spec((1, D_MODEL)),
                  _const_spec((D_MODEL, IN_WIDTH_P)),
                  tab, tab,
                  _const_spec((CONV_W, QKV_A)),
                  _const_spec((1, LANES)),
                  _const_spec((1, LANES))],
        out_specs=out_specs,
        out_shape=out_shape,
        compiler_params=pltpu.CompilerParams(dimension_semantics=("parallel",),
                                             vmem_limit_bytes=VMEM_BIG),
        name="inproj_latent" if latent else "inproj_context",
    )(x, x, x, mod4, mod4, norm_w, w_mix, cos_t, sin_t, conv_w, alog_p, dtb_p)


TRI_BASE = 16


def _unit_tri_inverses(lows):
    n = lows[0].shape[0]
    row = lax.broadcasted_iota(jnp.int32, (n, n), 0)
    col = lax.broadcasted_iota(jnp.int32, (n, n), 1)
    base = row // TRI_BASE == col // TRI_BASE
    eye = jnp.where(row == col, 1.0, 0.0)
    ps = [-jnp.where(base, low, 0.0) for low in lows]
    ts = [eye + p for p in ps]
    ps = [_dot(p, p) for p in ps]
    for _ in range(int(np.log2(TRI_BASE)) - 2):
        both = [_dot(jnp.concatenate([p.astype(BF), t.astype(BF)], axis=0), p) for p, t in zip(ps, ts)]
        ts = [t + b[n:] for t, b in zip(ts, both)]
        ps = [b[:n] for b in both]
    ts = [t + _dot(t, p) for t, p in zip(ts, ps)]
    size = TRI_BASE
    while size < n:
        pair = (row // (2 * size) == col // (2 * size)) & (row // size != col // size)
        mids = [_dot(jnp.where(pair, low, 0.0), t) for low, t in zip(lows, ts)]
        ts = [t - _dot(t, m) for t, m in zip(ts, mids)]
        size *= 2
    return ts


def _bf16_terms(x):
    hi = x.astype(BF)
    rest = x - hi.astype(F32)
    mid = rest.astype(BF)
    lo = (rest - mid.astype(F32)).astype(BF)
    return hi, mid, lo


def _scan_kernel(*refs, steps, chunks, has_init, emit_state):
    xf_ref, xb_ref, gf_ref, gb_ref = refs[:4]
    rest = list(refs[4:])
    s0_refs = (rest.pop(0), rest.pop(0)) if has_init else None
    of_ref, ob_ref = rest.pop(0), rest.pop(0)
    out_state_refs = (rest.pop(0), rest.pop(0)) if emit_state else None
    state, = rest
    c = pl.program_id(1)

    @pl.when(c == 0)
    def _():
        if has_init:
            state[0:H_A] = s0_refs[0][0]
            state[H_A:2 * H_A] = s0_refs[1][0]
        else:
            state[...] = jnp.zeros(state.shape, F32)

    row = lax.broadcasted_iota(jnp.int32, (CHUNK, CHUNK), 0)
    col = lax.broadcasted_iota(jnp.int32, (CHUNK, CHUNK), 1)
    x_refs, g_refs, o_refs = (xf_ref, xb_ref), (gf_ref, gb_ref), (of_ref, ob_ref)
    incl = (row >= col, row <= col)
    strict = (row > col, row < col)
    tri = [jnp.where(m, 1.0, 0.0).astype(BF) for m in incl]
    chains = [(d, hh) for d in range(2) for hh in range(H_A)]
    nch = len(chains)

    def one_chunk(sub, carry):
        rows = (pl.ds(pl.multiple_of(sub * CHUNK, CHUNK), CHUNK),
                pl.ds(pl.multiple_of((chunks - 1 - sub) * CHUNK, CHUNK), CHUNK))
        gbeta = [g_refs[d][rows[d], :] for d in range(2)]
        gc = [sum(jnp.dot(tri[d], part, preferred_element_type=F32) for part in _bf16_terms(gbeta[d]))
              for d in range(2)]
        gct = [g.T for g in gc]
        q = [x_refs[d][rows[d], hh * DK_A:(hh + 1) * DK_A] for d, hh in chains]
        k = [x_refs[d][rows[d], (H_A + hh) * DK_A:(H_A + hh + 1) * DK_A] for d, hh in chains]
        v = [x_refs[d][rows[d], (2 * H_A + hh) * DK_A:(2 * H_A + hh + 1) * DK_A] for d, hh in chains]
        gcol = [gc[d][:, d * H_A + hh:d * H_A + hh + 1] for d, hh in chains]
        beta = [gbeta[d][:, DEC_A + d * H_A + hh:DEC_A + d * H_A + hh + 1] for d, hh in chains]
        decay = [jnp.exp(jnp.where(incl[d], gcol[i] - gct[d][d * H_A + hh:d * H_A + hh + 1, :], NEG))
                 for i, (d, hh) in enumerate(chains)]
        qk_kk = [_dot_nt(jnp.concatenate([q[i].astype(BF), k[i].astype(BF)], axis=0), k[i]) for i in range(nch)]
        tmat = _unit_tri_inverses([jnp.where(strict[d], qk_kk[i][CHUNK:] * beta[i] * decay[i], 0.0)
                                   for i, (d, hh) in enumerate(chains)])
        egc = [jnp.exp(g) for g in gcol]
        uw = [_dot(tmat[i], jnp.concatenate([v[i] * beta[i], k[i] * (beta[i] * egc[i])], axis=1))
              for i in range(nch)]
        g_last = [gcol[i][CHUNK - 1:CHUNK] if d == 0 else gcol[i][0:1] for i, (d, hh) in enumerate(chains)]
        s_old = [state[i] for i in range(nch)]
        s_bf = [s.astype(BF) for s in s_old]
        ws_qs = [_dot(jnp.concatenate([uw[i][:, DV_A:].astype(BF), (q[i] * egc[i]).astype(BF)], axis=0), s_bf[i])
                 for i in range(nch)]
        v_new = [uw[i][:, :DV_A] - ws_qs[i][:CHUNK] for i in range(nch)]
        kd_t = [(k[i] * jnp.exp(g_last[i] - gcol[i])).T for i in range(nch)]
        av_kv = [_dot(jnp.concatenate([(qk_kk[i][:CHUNK] * decay[i]).astype(BF), kd_t[i].astype(BF)], axis=0),
                      v_new[i]) for i in range(nch)]
        for i, (d, hh) in enumerate(chains):
            o_refs[d][rows[d], hh * DV_A:(hh + 1) * DV_A] = ws_qs[i][CHUNK:] + av_kv[i][:CHUNK]
        for i in range(nch):
            state[i] = s_old[i] * jnp.exp(g_last[i]) + av_kv[i][CHUNK:]
        return carry

    lax.fori_loop(0, chunks, one_chunk, 0, unroll=True)

    if emit_state:
        @pl.when(c == steps - 1)
        def _():
            out_state_refs[0][0] = state[0:H_A]
            out_state_refs[1][0] = state[H_A:2 * H_A]


def _delta_scan(qkvc, gbeta, init, nb, seq, emit_state):
    n = qkvc.shape[0]
    chunks = min(SCAN_CHUNKS_PER_STEP, seq // CHUNK)
    rows = chunks * CHUNK
    steps = seq // rows
    fwd = lambda w: pl.BlockSpec((rows, w), lambda b, c: (b * steps + c, 0))
    bwd = lambda w: pl.BlockSpec((rows, w), lambda b, c: (b * steps + steps - 1 - c, 0))
    st = pl.BlockSpec((1, H_A, DK_A, DV_A), lambda b, c: (b, 0, 0, 0))
    st_shape = jax.ShapeDtypeStruct((nb, H_A, DK_A, DV_A), F32)
    o_shape = jax.ShapeDtypeStruct((n, Z_A), F32)
    has_init = init is not None
    return pl.pallas_call(
        functools.partial(_scan_kernel, steps=steps, chunks=chunks, has_init=has_init, emit_state=emit_state),
        grid=(nb, steps),
        in_specs=[fwd(QKV_A), bwd(QKV_A), fwd(LANES), bwd(LANES)] + ([st, st] if has_init else []),
        out_specs=[fwd(Z_A), bwd(Z_A)] + ([st, st] if emit_state else []),
        out_shape=[o_shape, o_shape] + ([st_shape, st_shape] if emit_state else []),
        scratch_shapes=[pltpu.VMEM((2 * H_A, DK_A, DV_A), F32)],
        compiler_params=pltpu.CompilerParams(dimension_semantics=("parallel", "arbitrary")),
        name="delta_scan",
    )(qkvc, qkvc, gbeta, gbeta, *(init if has_init else ()))


def _attn_core(q, keys, vals, valid, sink_ref, write):
    nq = q.shape[0]
    group = H_B // KV_B
    klane = lax.broadcasted_iota(jnp.int32, keys.shape, 1)
    krot = pltpu.roll(keys, HD_B, 1)
    qlow = lax.broadcasted_iota(jnp.int32, (nq, LANES), 1) < HD_B
    v_bf = vals.astype(BF)
    kdup = [jnp.where(klane < HD_B, keys, krot).astype(BF), jnp.where(klane < HD_B, krot, keys).astype(BF)]
    scores = []
    for head in range(H_B):
        qt = q[:, (head // 2) * LANES:(head // 2 + 1) * LANES]
        qh = jnp.where(qlow, qt, 0.0) if head % 2 == 0 else jnp.where(qlow, 0.0, qt)
        scores.append(_dot_nt(qh, kdup[head // group]))
    probs, inv_den = [], []
    for head, s in enumerate(scores):
        if valid is not None:
            s = jnp.where(valid, s, NEG)
        snk = sink_ref[head]
        m = jnp.maximum(jnp.max(s, axis=-1, keepdims=True), snk)
        p = jnp.exp(s - m)
        inv_den.append(1.0 / (jnp.sum(p, axis=-1, keepdims=True) + jnp.exp(snk - m)))
        probs.append(p.astype(BF))
    o_all = [jnp.dot(jnp.concatenate(probs[g * group:(g + 1) * group], axis=0), v_bf,
                     preferred_element_type=F32) for g in range(KV_B)]
    for g in range(KV_B):
        for t in range(group // 2):
            a = o_all[g][(2 * t) * nq:(2 * t + 1) * nq] * inv_den[g * group + 2 * t]
            b = o_all[g][(2 * t + 1) * nq:(2 * t + 2) * nq] * inv_den[g * group + 2 * t + 1]
            if g == 0:
                tile = jnp.where(qlow, a, pltpu.roll(b, HD_B, 1))
            else:
                tile = jnp.where(qlow, pltpu.roll(a, HD_B, 1), b)
            write((g * group // 2 + t) * LANES, tile)


def _ctx_attn_kernel(sink_ref, q_ref, k_ref, v_ref, o_ref):
    def write(col, tile):
        o_ref[:, col:col + LANES] = tile

    _attn_core(q_ref[...] * HD_B ** -0.5, k_ref[...], v_ref[...], None, sink_ref, write)


def _context_attention(q, k, v, sink, nb, seq):
    n = q.shape[0]
    blk = lambda w: pl.BlockSpec((seq, w), lambda b: (b, 0))
    return pl.pallas_call(
        _ctx_attn_kernel,
        grid=(nb,),
        in_specs=[pl.BlockSpec(memory_space=pltpu.SMEM), blk(Q_B), blk(K_B), blk(V_B)],
        out_specs=blk(Q_B),
        out_shape=jax.ShapeDtypeStruct((n, Q_B), F32),
        compiler_params=pltpu.CompilerParams(dimension_semantics=("parallel",)),
        name="context_attention",
    )(sink, q, k, v)


def _win_attn_kernel(sink_ref, q_ref, k_ref, v_ref, kx_ref, vx_ref, o_ref):
    step = pl.program_id(1)
    seq = k_ref.shape[0]
    n_local = 3 * WINDOW
    nk = n_local + kx_ref.shape[1]
    r = lax.broadcasted_iota(jnp.int32, (WINDOW, nk), 0)
    j = lax.broadcasted_iota(jnp.int32, (WINDOW, nk), 1)

    def block(sub, carry):
        i = step * WIN_SUB + sub
        start = pl.multiple_of(jnp.clip((i - 1) * WINDOW, 0, seq - n_local), WINDOW)
        keys = jnp.concatenate([k_ref[pl.ds(start, n_local), :], kx_ref[0]], axis=0)
        vals = jnp.concatenate([v_ref[pl.ds(start, n_local), :], vx_ref[0]], axis=0)
        dist = (i * WINDOW - start) + r - j
        valid = ((dist <= WINDOW) & (dist >= -WINDOW)) | (j >= n_local)
        rows = pl.ds(pl.multiple_of(sub * WINDOW, WINDOW), WINDOW)

        def write(col, tile):
            o_ref[rows, col:col + LANES] = tile

        _attn_core(q_ref[rows, :] * HD_B ** -0.5, keys, vals, valid, sink_ref, write)
        return carry

    lax.fori_loop(0, WIN_SUB, block, 0, unroll=True)


def _window_attention(q, k, v, k_ctx, v_ctx, sink, nb, seq):
    n = q.shape[0]
    tq = WIN_SUB * WINDOW
    steps = seq // tq
    n_ctx = k_ctx.shape[1]
    qblk = pl.BlockSpec((tq, Q_B), lambda b, i: (b * steps + i, 0))
    kv = pl.BlockSpec((seq, K_B), lambda b, i: (b, 0))
    ctx = pl.BlockSpec((1, n_ctx, K_B), lambda b, i: (b, 0, 0))
    return pl.pallas_call(
        _win_attn_kernel,
        grid=(nb, steps),
        in_specs=[pl.BlockSpec(memory_space=pltpu.SMEM), qblk, kv, kv, ctx, ctx],
        out_specs=qblk,
        out_shape=jax.ShapeDtypeStruct((n, Q_B), F32),
        compiler_params=pltpu.CompilerParams(dimension_semantics=("parallel", "parallel")),
        name="window_attention",
    )(sink, q, k, v, k_ctx, v_ctx)


def _merge_kernel(x_ref, of_ref, ob_ref, yb_ref, sh_ref, sc_ref, g2_ref, nw_ref, on_ref,
                  wzg_ref, woa_ref, wob_ref, wout_ref, o_ref):
    x = x_ref[...]
    h = _modulated_norm(x, nw_ref, sh_ref, sc_ref)
    z = jnp.dot(h, wzg_ref[:, :Z_A], preferred_element_type=F32)
    o = of_ref[...] + ob_ref[...]
    heads = []
    for hh in range(H_A):
        sl = slice(hh * DV_A, (hh + 1) * DV_A)
        heads.append(_rms(o[:, sl], on_ref[...]) * _silu(z[:, sl]))
    y_a = jnp.concatenate(heads, axis=1)
    gate_a = jnp.dot(h, wzg_ref[:, Z_A:Z_A + D_MODEL], preferred_element_type=F32)
    gate_b = jnp.dot(h, wzg_ref[:, Z_A + D_MODEL:], preferred_element_type=F32)
    m = _sigmoid(gate_a) * _dot(y_a, woa_ref[...]) + _sigmoid(gate_b) * _dot(yb_ref[...], wob_ref[...])
    o_ref[...] = x + g2_ref[0, 0] * _dot(m, wout_ref[...])


def _merge(x, o_f, o_b, y_b, mod4, tokens_per_row, row0, norm_w, onorm, w_zg, w_oa, w_ob, w_out):
    n = x.shape[0]
    tm = MERGE_TILE
    tok = lambda w: pl.BlockSpec((tm, w), lambda i: (i, 0))
    return pl.pallas_call(
        _merge_kernel,
        grid=(n // tm,),
        in_specs=[tok(D_MODEL), tok(Z_A), tok(Z_A), tok(Q_B),
                  _mod_spec(3, tm, tokens_per_row, row0),
                  _mod_spec(4, tm, tokens_per_row, row0),
                  _mod_spec(5, tm, tokens_per_row, row0),
                  _const_spec((1, D_MODEL)),
                  _const_spec((1, DV_A)),
                  _const_spec((D_MODEL, Z_A + GATES)),
                  _const_spec((H_A * DV_A, D_MODEL)),
                  _const_spec((H_B * HD_B, D_MODEL)),
                  _const_spec((D_MODEL, D_MODEL))],
        out_specs=tok(D_MODEL),
        out_shape=jax.ShapeDtypeStruct((n, D_MODEL), F32),
        compiler_params=pltpu.CompilerParams(dimension_semantics=("parallel",),
                                             vmem_limit_bytes=VMEM_BIG),
        name="merge",
    )(x, o_f, o_b, y_b, mod4, mod4, mod4, norm_w, onorm, w_zg, w_oa, w_ob, w_out)


def _rope_tables(seq):
    rows = seq // GRID_W
    row = jnp.repeat(jnp.arange(rows, dtype=F32), GRID_W)
    col = jnp.tile(jnp.arange(GRID_W, dtype=F32), rows)
    inv = jnp.power(ROPE_BASE, -jnp.arange(ROPE_AXIS_PAIRS, dtype=F32) / ROPE_AXIS_PAIRS)
    ang = jnp.concatenate([row[:, None] * inv, col[:, None] * inv], axis=-1)
    cos, sin = jnp.cos(ang), jnp.sin(ang)
    cos_t = jnp.tile(cos, (1, LANES // (HD_B // 2)))
    sin_t = jnp.tile(jnp.concatenate([-sin, sin], axis=-1), (1, LANES // HD_B))
    return cos_t, sin_t


def _split_w_in(w_in):
    edges = np.cumsum((0, QKV_A, Z_A, DEC_A, BETA_A, Q_B, K_B, V_B, GATES))
    part = lambda i: w_in[:, int(edges[i]):int(edges[i + 1])].astype(BF)
    pad = jnp.zeros((D_MODEL, LANES - DEC_A - BETA_A), BF)
    w_mix = jnp.concatenate([part(0), part(4), part(5), part(6), part(2), part(3), pad], axis=1)
    w_zg = jnp.concatenate([part(1), part(7)], axis=1)
    return w_mix, w_zg


def _cache_layout(t, nb, seq):
    return t.reshape(nb, KV_B, HD_B, seq).transpose(0, 3, 1, 2)[:, None]


def kernel(x_prompt, x_sample, state_delta_fwd, state_delta_bwd, cache_k, cache_v, c, c_ctx, ada_w, ada_b, norm_ffn1, ffn1_w13, ffn1_w2, norm_mix, w_in, conv_w, a_log, dt_bias, onorm_a, w_oa, w_ob, w_out, sink, norm_ffn2, ffn2_w13, ffn2_w2, norm_final):
    bp, sp = x_prompt.shape[:2]
    bs, ts = x_sample.shape[:2]
    layer = 0

    cvec = jnp.concatenate([c_ctx[None, :], c, jnp.zeros((MOD_ROWS - 1 - bs, D_MODEL), F32)], axis=0)
    mod4 = _modulation(cvec, ada_w[layer], ada_b[layer]).reshape(MOD_ROWS, N_MOD, 1, D_MODEL)

    row = lambda a: a.reshape(1, -1)
    lane_pad = lambda a: jnp.pad(a.reshape(1, -1), ((0, 0), (0, LANES - a.size)))
    w13_1, w2_1 = ffn1_w13[layer].astype(BF), ffn1_w2[layer].astype(BF)
    w13_2, w2_2 = ffn2_w13[layer].astype(BF), ffn2_w2[layer].astype(BF)
    w_mix, w_zg = _split_w_in(w_in[layer])
    w_oa_b, w_ob_b, w_out_b = w_oa[layer].astype(BF), w_ob[layer].astype(BF), w_out[layer].astype(BF)
    alog_p, dtb_p = lane_pad(a_log[layer]), lane_pad(dt_bias[layer])
    cos_t, sin_t = _rope_tables(ts)
    final_w = row(norm_final)

    def run(x, nb, seq, tokens_per_row, row0, init, ctx):
        latent = ctx is not None
        x1 = _ffn(x, mod4, (0, 1, 2), tokens_per_row, row0, row(norm_ffn1[layer]), w13_1, w2_1,
                  final_w, final=False)
        mix = _inproj(x1, mod4, tokens_per_row, row0, row(norm_mix[layer]), w_mix, cos_t, sin_t,
                      conv_w[layer], alog_p, dtb_p, seq, latent)
        qkvc, qb, kb, vb, gbeta = mix[:5]
        scan = _delta_scan(qkvc, gbeta, init, nb, seq, emit_state=not latent)
        if latent:
            y_b = _window_attention(qb, kb, vb, ctx[0], ctx[1], sink[layer], nb, seq)
        else:
            y_b = _context_attention(qb, kb, vb, sink[layer], nb, seq)
        x2 = _merge(x1, scan[0], scan[1], y_b, mod4, tokens_per_row, row0, row(norm_mix[layer]),
                    row(onorm_a[layer]), w_zg, w_oa_b, w_ob_b, w_out_b)
        y = _ffn(x2, mod4, (6, 7, 8), tokens_per_row, row0, row(norm_ffn2[layer]), w13_2, w2_2,
                 final_w, final=True)
        return y, scan[2:], mix[5:]

    yp, (s_f, s_b), (k_t, v_t) = run(x_prompt.reshape(bp * sp, D_MODEL), bp, sp, bp * sp, 0, None, None)
    ctx = (cache_k[:, layer].reshape(bs, -1, K_B), cache_v[:, layer].reshape(bs, -1, V_B))
    ys, _, _ = run(x_sample.reshape(bs * ts, D_MODEL), bs, ts, ts, 1,
                   (state_delta_fwd[:, layer], state_delta_bwd[:, layer]), ctx)

    return (yp.reshape(bp, sp, D_MODEL), ys.reshape(bs, ts, D_MODEL),
            s_f[:, None], s_b[:, None], _cache_layout(k_t, bp, sp), _cache_layout(v_t, bp, sp))
```

```python
import functools

import jax
import jax.numpy as jnp
import numpy as np
from jax import lax
from jax.experimental import pallas as pl
from jax.experimental.pallas import tpu as pltpu

D_MODEL = 1024
GRID_W = 64
H_A = 4
DK_A = 128
DV_A = 128
CONV_W = 5
H_B = 8
KV_B = 2
HD_B = 64
WINDOW = 128
ROPE_BASE = 10000.0
ROPE_AXIS_PAIRS = HD_B // 4
D_FF = 2816
HALF_STEP = 0.5
N_MOD = 9
EPS = 1e-6

QKV_A = 2 * H_A * DK_A + H_A * DV_A
Z_A = H_A * DV_A
DEC_A = 2 * H_A
BETA_A = 2 * H_A
Q_B = H_B * HD_B
K_B = KV_B * HD_B
V_B = KV_B * HD_B
GATES = 2 * D_MODEL

LANES = 128
SUBLANES = 8
V7X_VMEM_BYTES = 64 * 1024 * 1024
NEG = -1e30

BF = jnp.bfloat16
F32 = jnp.float32

FFN_TILE = 1024
FFN_CHUNK = 256
IN_TILE_MAX = 1024
MERGE_TILE = 512
MOD_COLS = N_MOD * D_MODEL // 8
MOD_ROWS = 16
CHUNK = 128
SCAN_CHUNKS_PER_STEP = 4
WIN_SUB = 8
VMEM_BIG = V7X_VMEM_BYTES // 8 * 7


def _dot(a, b):
    return jnp.dot(a.astype(BF), b.astype(BF), preferred_element_type=F32)


def _dot_nt(a, b):
    return lax.dot_general(a.astype(BF), b.astype(BF), (((1,), (1,)), ((), ())),
                           preferred_element_type=F32)


def _sigmoid(x):
    return 0.5 + 0.5 * jnp.tanh(0.5 * x)


def _silu(x):
    half = 0.5 * x
    return half + half * jnp.tanh(half)


def _rms(x, w):
    return x * lax.rsqrt(jnp.mean(x * x, axis=-1, keepdims=True) + EPS) * w


def _modulated_norm(x, nw_ref, sh_ref, sc_ref):
    gain = nw_ref[...] * (1.0 + sc_ref[0, 0])
    xn = x * lax.rsqrt(jnp.mean(x * x, axis=-1, keepdims=True) + EPS)
    return (xn * gain + sh_ref[0, 0]).astype(BF)


def _const_spec(shape):
    nd = len(shape)
    return pl.BlockSpec(shape, lambda *_: (0,) * nd, pipeline_mode=pl.Buffered(1))


def _mod_spec(kind, tm, tokens_per_row, row0):
    return pl.BlockSpec((1, 1, 1, D_MODEL),
                        lambda i: (row0 + (i * tm) // tokens_per_row, kind, 0, 0))


def _mod_kernel(c_ref, w_ref, b_ref, o_ref):
    o_ref[...] = _dot(_silu(c_ref[...]), w_ref[...]) + b_ref[...]


def _modulation(cvec, ada_w, ada_b):
    rows = cvec.shape[0]
    n = N_MOD * D_MODEL
    bn = MOD_COLS
    return pl.pallas_call(
        _mod_kernel,
        grid=(n // bn,),
        in_specs=[pl.BlockSpec((rows, D_MODEL), lambda j: (0, 0)),
                  pl.BlockSpec((D_MODEL, bn), lambda j: (0, j)),
                  pl.BlockSpec((1, bn), lambda j: (0, j))],
        out_specs=pl.BlockSpec((rows, bn), lambda j: (0, j)),
        out_shape=jax.ShapeDtypeStruct((rows, n), F32),
        name="modulation",
    )(cvec, ada_w, ada_b.reshape(1, n))


def _ffn_kernel(x_ref, sh_ref, sc_ref, g_ref, nw_ref, w13_ref, w2_ref, fw_ref, o_ref, *, ff_chunk, final):
    x = x_ref[...]
    h = _modulated_norm(x, nw_ref, sh_ref, sc_ref)
    acc = jnp.zeros(x.shape, F32)
    for j in range(D_FF // ff_chunk):
        a = jnp.dot(h, w13_ref[:, j * ff_chunk:(j + 1) * ff_chunk], preferred_element_type=F32)
        b = jnp.dot(h, w13_ref[:, D_FF + j * ff_chunk:D_FF + (j + 1) * ff_chunk],
                    preferred_element_type=F32)
        t = (_silu(a) * b).astype(BF)
        acc = acc + jnp.dot(t, w2_ref[j * ff_chunk:(j + 1) * ff_chunk, :], preferred_element_type=F32)
    y = x + (HALF_STEP * g_ref[0, 0]) * acc
    if final:
        y = _rms(y, fw_ref[...])
    o_ref[...] = y


def _ffn(x, mod4, kinds, tokens_per_row, row0, norm_w, w13, w2, final_w, final):
    n = x.shape[0]
    tm = FFN_TILE
    tok = pl.BlockSpec((tm, D_MODEL), lambda i: (i, 0))
    return pl.pallas_call(
        functools.partial(_ffn_kernel, ff_chunk=FFN_CHUNK, final=final),
        grid=(n // tm,),
        in_specs=[tok,
                  _mod_spec(kinds[0], tm, tokens_per_row, row0),
                  _mod_spec(kinds[1], tm, tokens_per_row, row0),
                  _mod_spec(kinds[2], tm, tokens_per_row, row0),
                  _const_spec((1, D_MODEL)),
                  _const_spec((D_MODEL, 2 * D_FF)),
                  _const_spec((D_FF, D_MODEL)),
                  _const_spec((1, D_MODEL))],
        out_specs=tok,
        out_shape=jax.ShapeDtypeStruct((n, D_MODEL), F32),
        compiler_params=pltpu.CompilerParams(dimension_semantics=("parallel",),
                                             vmem_limit_bytes=VMEM_BIG),
        name="ffn_final" if final else "ffn",
    )(x, mod4, mod4, mod4, norm_w, w13, w2, final_w)


IN_WIDTH_P = QKV_A + Q_B + K_B + V_B + LANES
HALO = SUBLANES


def _swap_halves(x):
    w = x.shape[-1]
    lane = lax.broadcasted_iota(jnp.int32, x.shape, 1)
    lower = (lane % HD_B) < (HD_B // 2)
    return jnp.where(lower, pltpu.roll(x, w - HD_B // 2, 1), pltpu.roll(x, HD_B // 2, 1))


def _rope(p, cos_ref, sin_ref):
    reps = p.shape[1] // LANES
    c = jnp.concatenate([cos_ref[...]] * reps, axis=1) if reps > 1 else cos_ref[...]
    s = jnp.concatenate([sin_ref[...]] * reps, axis=1) if reps > 1 else sin_ref[...]
    return p * c + _swap_halves(p) * s


def _inproj_kernel(xm_ref, xp_ref, xn_ref, sh_ref, sc_ref, nw_ref, w_ref, cos_ref, sin_ref,
                   cw_ref, alog_ref, dtb_ref, qkv_ref, qb_ref, kb_ref, vb_ref, gbeta_ref, *rest,
                   latent, tiles):
    t = pl.program_id(0) % tiles
    x = jnp.concatenate([xp_ref[...], xm_ref[...], xn_ref[...]], axis=0)
    h = _modulated_norm(x, nw_ref, sh_ref, sc_ref)
    rows = x.shape[0]
    tile = rows - 2 * HALO
    blocks = rows // SUBLANES
    hm = h[HALO:HALO + tile]

    def project(first, width, lhs=hm):
        return jnp.dot(lhs, w_ref[:, first:first + width], preferred_element_type=F32)

    def conv_section(raw, first):
        width = raw.shape[1]
        r3 = raw.reshape(blocks, SUBLANES, width)
        r3 = jnp.concatenate([jnp.where(t == 0, 0.0, r3[0:1]), r3[1:blocks - 1],
                              jnp.where(t == tiles - 1, 0.0, r3[blocks - 1:blocks])], axis=0)
        sub = lax.broadcasted_iota(jnp.int32, (1, SUBLANES, width), 1)
        acc = None
        for j in range(CONV_W):
            s = CONV_W // 2 - j
            tap = cw_ref[j:j + 1, first:first + width].reshape(1, 1, width)
            if s == 0:
                sh = r3[1:blocks - 1]
            else:
                if s > 0:
                    m = jnp.where(sub < SUBLANES - s, r3[1:blocks - 1], r3[0:blocks - 2])
                else:
                    m = jnp.where(sub >= -s, r3[1:blocks - 1], r3[2:blocks])
                sh = pltpu.roll(m, s % SUBLANES, 1)
            acc = sh * tap if acc is None else acc + sh * tap
        y = _silu(acc.reshape(tile, width))
        for hh in range(width // DK_A):
            v = y[:, hh * DK_A:(hh + 1) * DK_A]
            if first < 2 * H_A * DK_A:
                scale = DK_A ** -0.5 if first < H_A * DK_A else 1.0
                v = v * (lax.rsqrt(jnp.sum(v * v, axis=-1, keepdims=True) + EPS) * scale)
            qkv_ref[:, first + hh * DK_A:first + (hh + 1) * DK_A] = v

    sec = H_A * DK_A
    qb0 = QKV_A
    kv0 = qb0 + Q_B
    db0 = kv0 + K_B + V_B
    for i in range(QKV_A // sec):
        conv_section(project(i * sec, sec, h), i * sec)
    qb = project(qb0, Q_B)
    qb_ref[...] = _rope(qb, cos_ref, sin_ref) if latent else qb
    kv = project(kv0, K_B + V_B)
    kb_ref[...] = _rope(kv[:, :K_B], cos_ref, sin_ref) if latent else kv[:, :K_B]
    vb_ref[...] = kv[:, K_B:]
    if not latent:
        kt_ref, vt_ref = rest
        kt_ref[0] = kv[:, :K_B].T
        vt_ref[0] = kv[:, K_B:].T
    db = project(db0, LANES)

    xg = db + dtb_ref[...]
    softplus = jnp.maximum(xg, 0.0) + jnp.log1p(jnp.exp(-jnp.abs(xg)))
    g = -jnp.exp(alog_ref[...]) * softplus
    lane = lax.broadcasted_iota(jnp.int32, db.shape, 1)
    gbeta_ref[...] = jnp.where(lane < DEC_A, g, jnp.where(lane < DEC_A + BETA_A, _sigmoid(db), 0.0))


def _inproj(x, mod4, tokens_per_row, row0, norm_w, w_mix, cos_t, sin_t, conv_w, alog_p, dtb_p, seq, latent):
    n = x.shape[0]
    tm = min(IN_TILE_MAX, seq)
    tiles = seq // tm
    halos_per_tile = tm // HALO
    last_halo = n // HALO - 1
    tok = lambda w: pl.BlockSpec((tm, w), lambda i: (i, 0))
    prev = pl.BlockSpec((HALO, D_MODEL), lambda i: (jnp.maximum(i * halos_per_tile - 1, 0), 0))
    nxt = pl.BlockSpec((HALO, D_MODEL), lambda i: (jnp.minimum((i + 1) * halos_per_tile, last_halo), 0))
    tab = pl.BlockSpec((tm, LANES), lambda i: (i % tiles, 0))
    widths = (QKV_A, Q_B, K_B, V_B, LANES)
    out_specs = [tok(w) for w in widths]
    out_shape = [jax.ShapeDtypeStruct((n, w), F32) for w in widths]
    if not latent:
        assert tiles == 1
        out_specs += [pl.BlockSpec((1, K_B, tm), lambda i: (i, 0, 0))] * 2
        out_shape += [jax.ShapeDtypeStruct((n // tm, K_B, tm), F32)] * 2
    return pl.pallas_call(
        functools.partial(_inproj_kernel, latent=latent, tiles=tiles),
        grid=(n // tm,),
        in_specs=[tok(D_MODEL), prev, nxt,
                  _mod_spec(3, tm, tokens_per_row, row0),
                  _mod_spec(4, tm, tokens_per_row, row0),
                  _const_spec((1, D_MODEL)),
                  _const_spec((D_MODEL, IN_WIDTH_P)),
                  tab, tab,
                  _const_spec((CONV_W, QKV_A)),
                  _const_spec((1, LANES)),
                  _const_spec((1, LANES))],
        out_specs=out_specs,
        out_shape=out_shape,
        compiler_params=pltpu.CompilerParams(dimension_semantics=("parallel",),
                                             vmem_limit_bytes=VMEM_BIG),
        name="inproj_latent" if latent else "inproj_context",
    )(x, x, x, mod4, mod4, norm_w, w_mix, cos_t, sin_t, conv_w, alog_p, dtb_p)


TRI_BASE = 16


def _unit_tri_inverses(lows):
    n = lows[0].shape[0]
    row = lax.broadcasted_iota(jnp.int32, (n, n), 0)
    col = lax.broadcasted_iota(jnp.int32, (n, n), 1)
    base = row // TRI_BASE == col // TRI_BASE
    eye = jnp.where(row == col, 1.0, 0.0)
    ps = [-jnp.where(base, low, 0.0) for low in lows]
    ts = [eye + p for p in ps]
    ps = [_dot(p, p) for p in ps]
    for _ in range(int(np.log2(TRI_BASE)) - 2):
        both = [_dot(jnp.concatenate([p.astype(BF), t.astype(BF)], axis=0), p) for p, t in zip(ps, ts)]
        ts = [t + b[n:] for t, b in zip(ts, both)]
        ps = [b[:n] for b in both]
    ts = [t + _dot(t, p) for t, p in zip(ts, ps)]
    size = TRI_BASE
    while size < n:
        pair = (row // (2 * size) == col // (2 * size)) & (row // size != col // size)
        mids = [_dot(jnp.where(pair, low, 0.0), t) for low, t in zip(lows, ts)]
        ts = [t - _dot(t, m) for t, m in zip(ts, mids)]
        size *= 2
    return ts


def _bf16_terms(x):
    hi = x.astype(BF)
    rest = x - hi.astype(F32)
    mid = rest.astype(BF)
    lo = (rest - mid.astype(F32)).astype(BF)
    return hi, mid, lo


def _scan_kernel(*refs, steps, chunks, has_init, emit_state):
    xf_ref, xb_ref, gf_ref, gb_ref = refs[:4]
    rest = list(refs[4:])
    s0_refs = (rest.pop(0), rest.pop(0)) if has_init else None
    of_ref, ob_ref = rest.pop(0), rest.pop(0)
    out_state_refs = (rest.pop(0), rest.pop(0)) if emit_state else None
    state, = rest
    c = pl.program_id(1)

    @pl.when(c == 0)
    def _():
        if has_init:
            state[0:H_A] = s0_refs[0][0]
            state[H_A:2 * H_A] = s0_refs[1][0]
        else:
            state[...] = jnp.zeros(state.shape, F32)

    row = lax.broadcasted_iota(jnp.int32, (CHUNK, CHUNK), 0)
    col = lax.broadcasted_iota(jnp.int32, (CHUNK, CHUNK), 1)
    x_refs, g_refs, o_refs = (xf_ref, xb_ref), (gf_ref, gb_ref), (of_ref, ob_ref)
    incl = (row >= col, row <= col)
    strict = (row > col, row < col)
    tri = [jnp.where(m, 1.0, 0.0).astype(BF) for m in incl]
    chains = [(d, hh) for d in range(2) for hh in range(H_A)]
    nch = len(chains)

    def one_chunk(sub, carry):
        rows = (pl.ds(pl.multiple_of(sub * CHUNK, CHUNK), CHUNK),
                pl.ds(pl.multiple_of((chunks - 1 - sub) * CHUNK, CHUNK), CHUNK))
        gbeta = [g_refs[d][rows[d], :] for d in range(2)]
        gc = [sum(jnp.dot(tri[d], part, preferred_element_type=F32) for part in _bf16_terms(gbeta[d]))
              for d in range(2)]
        gct = [g.T for g in gc]
        q = [x_refs[d][rows[d], hh * DK_A:(hh + 1) * DK_A] for d, hh in chains]
        k = [x_refs[d][rows[d], (H_A + hh) * DK_A:(H_A + hh + 1) * DK_A] for d, hh in chains]
        v = [x_refs[d][rows[d], (2 * H_A + hh) * DK_A:(2 * H_A + hh + 1) * DK_A] for d, hh in chains]
        gcol = [gc[d][:, d * H_A + hh:d * H_A + hh + 1] for d, hh in chains]
        beta = [gbeta[d][:, DEC_A + d * H_A + hh:DEC_A + d * H_A + hh + 1] for d, hh in chains]
        decay = [jnp.exp(jnp.where(incl[d], gcol[i] - gct[d][d * H_A + hh:d * H_A + hh + 1, :], NEG))
                 for i, (d, hh) in enumerate(chains)]
        qk_kk = [_dot_nt(jnp.concatenate([q[i].astype(BF), k[i].astype(BF)], axis=0), k[i]) for i in range(nch)]
        tmat = _unit_tri_inverses([jnp.where(strict[d], qk_kk[i][CHUNK:] * beta[i] * decay[i], 0.0)
                                   for i, (d, hh) in enumerate(chains)])
        egc = [jnp.exp(g) for g in gcol]
        uw = [_dot(tmat[i], jnp.concatenate([v[i] * beta[i], k[i] * (beta[i] * egc[i])], axis=1))
              for i in range(nch)]
        g_last = [gcol[i][CHUNK - 1:CHUNK] if d == 0 else gcol[i][0:1] for i, (d, hh) in enumerate(chains)]
        s_old = [state[i] for i in range(nch)]
        s_bf = [s.astype(BF) for s in s_old]
        ws_qs = [_dot(jnp.concatenate([uw[i][:, DV_A:].astype(BF), (q[i] * egc[i]).astype(BF)], axis=0), s_bf[i])
                 for i in range(nch)]
        v_new = [uw[i][:, :DV_A] - ws_qs[i][:CHUNK] for i in range(nch)]
        kd_t = [(k[i] * jnp.exp(g_last[i] - gcol[i])).T for i in range(nch)]
        av_kv = [_dot(jnp.concatenate([(qk_kk[i][:CHUNK] * decay[i]).astype(BF), kd_t[i].astype(BF)], axis=0),
                      v_new[i]) for i in range(nch)]
        for i, (d, hh) in enumerate(chains):
            o_refs[d][rows[d], hh * DV_A:(hh + 1) * DV_A] = ws_qs[i][CHUNK:] + av_kv[i][:CHUNK]
        for i in range(nch):
            state[i] = s_old[i] * jnp.exp(g_last[i]) + av_kv[i][CHUNK:]
        return carry

    lax.fori_loop(0, chunks, one_chunk, 0, unroll=True)

    if emit_state:
        @pl.when(c == steps - 1)
        def _():
            out_state_refs[0][0] = state[0:H_A]
            out_state_refs[1][0] = state[H_A:2 * H_A]


def _delta_scan(qkvc, gbeta, init, nb, seq, emit_state):
    n = qkvc.shape[0]
    chunks = min(SCAN_CHUNKS_PER_STEP, seq // CHUNK)
    rows = chunks * CHUNK
    steps = seq // rows
    fwd = lambda w: pl.BlockSpec((rows, w), lambda b, c: (b * steps + c, 0))
    bwd = lambda w: pl.BlockSpec((rows, w), lambda b, c: (b * steps + steps - 1 - c, 0))
    st = pl.BlockSpec((1, H_A, DK_A, DV_A), lambda b, c: (b, 0, 0, 0))
    st_shape = jax.ShapeDtypeStruct((nb, H_A, DK_A, DV_A), F32)
    o_shape = jax.ShapeDtypeStruct((n, Z_A), F32)
    has_init = init is not None
    return pl.pallas_call(
        functools.partial(_scan_kernel, steps=steps, chunks=chunks, has_init=has_init, emit_state=emit_state),
        grid=(nb, steps),
        in_specs=[fwd(QKV_A), bwd(QKV_A), fwd(LANES), bwd(LANES)] + ([st, st] if has_init else []),
        out_specs=[fwd(Z_A), bwd(Z_A)] + ([st, st] if emit_state else []),
        out_shape=[o_shape, o_shape] + ([st_shape, st_shape] if emit_state else []),
        scratch_shapes=[pltpu.VMEM((2 * H_A, DK_A, DV_A), F32)],
        compiler_params=pltpu.CompilerParams(dimension_semantics=("parallel", "arbitrary")),
        name="delta_scan",
    )(qkvc, qkvc, gbeta, gbeta, *(init if has_init else ()))


LOG2E = float(np.log2(np.e))
ATTN_SCALE = HD_B ** -0.5 * LOG2E


def _attn_core(q, keys, vals, valid, sink_ref, write):
    nq = q.shape[0]
    group = H_B // KV_B
    klane = lax.broadcasted_iota(jnp.int32, keys.shape, 1)
    krot = pltpu.roll(keys, HD_B, 1)
    qlow = lax.broadcasted_iota(jnp.int32, (nq, LANES), 1) < HD_B
    v_bf = vals.astype(BF)
    kdup = [jnp.where(klane < HD_B, keys, krot).astype(BF), jnp.where(klane < HD_B, krot, keys).astype(BF)]
    scores = []
    for head in range(H_B):
        qt = q[:, (head // 2) * LANES:(head // 2 + 1) * LANES]
        qh = jnp.where(qlow, qt, 0.0) if head % 2 == 0 else jnp.where(qlow, 0.0, qt)
        scores.append(_dot_nt(qh, kdup[head // group]))
    probs, inv_den = [], []
    for head, s in enumerate(scores):
        if valid is not None:
            s = jnp.where(valid, s, NEG)
        snk = sink_ref[head] * LOG2E
        m = jnp.maximum(jnp.max(s, axis=-1, keepdims=True), snk)
        p = jnp.exp2(s - m)
        inv_den.append(1.0 / (jnp.sum(p, axis=-1, keepdims=True) + jnp.exp2(snk - m)))
        probs.append(p.astype(BF))
    o_all = [jnp.dot(jnp.concatenate(probs[g * group:(g + 1) * group], axis=0), v_bf,
                     preferred_element_type=F32) for g in range(KV_B)]
    for g in range(KV_B):
        for t in range(group // 2):
            a = o_all[g][(2 * t) * nq:(2 * t + 1) * nq] * inv_den[g * group + 2 * t]
            b = o_all[g][(2 * t + 1) * nq:(2 * t + 2) * nq] * inv_den[g * group + 2 * t + 1]
            if g == 0:
                tile = jnp.where(qlow, a, pltpu.roll(b, HD_B, 1))
            else:
                tile = jnp.where(qlow, pltpu.roll(a, HD_B, 1), b)
            write((g * group // 2 + t) * LANES, tile)


def _ctx_attn_kernel(sink_ref, q_ref, k_ref, v_ref, o_ref):
    def write(col, tile):
        o_ref[:, col:col + LANES] = tile

    _attn_core(q_ref[...] * ATTN_SCALE, k_ref[...], v_ref[...], None, sink_ref, write)


def _context_attention(q, k, v, sink, nb, seq):
    n = q.shape[0]
    blk = lambda w: pl.BlockSpec((seq, w), lambda b: (b, 0))
    return pl.pallas_call(
        _ctx_attn_kernel,
        grid=(nb,),
        in_specs=[pl.BlockSpec(memory_space=pltpu.SMEM), blk(Q_B), blk(K_B), blk(V_B)],
        out_specs=blk(Q_B),
        out_shape=jax.ShapeDtypeStruct((n, Q_B), F32),
        compiler_params=pltpu.CompilerParams(dimension_semantics=("parallel",)),
        name="context_attention",
    )(sink, q, k, v)


def _win_attn_kernel(sink_ref, q_ref, k_ref, v_ref, kx_ref, vx_ref, o_ref):
    step = pl.program_id(1)
    seq = k_ref.shape[0]
    n_local = 3 * WINDOW
    nk = n_local + kx_ref.shape[1]
    r = lax.broadcasted_iota(jnp.int32, (WINDOW, nk), 0)
    j = lax.broadcasted_iota(jnp.int32, (WINDOW, nk), 1)

    def block(sub, carry):
        i = step * WIN_SUB + sub
        start = pl.multiple_of(jnp.clip((i - 1) * WINDOW, 0, seq - n_local), WINDOW)
        keys = jnp.concatenate([k_ref[pl.ds(start, n_local), :], kx_ref[0]], axis=0)
        vals = jnp.concatenate([v_ref[pl.ds(start, n_local), :], vx_ref[0]], axis=0)
        dist = (i * WINDOW - start) + r - j
        valid = ((dist <= WINDOW) & (dist >= -WINDOW)) | (j >= n_local)
        rows = pl.ds(pl.multiple_of(sub * WINDOW, WINDOW), WINDOW)

        def write(col, tile):
            o_ref[rows, col:col + LANES] = tile

        _attn_core(q_ref[rows, :] * ATTN_SCALE, keys, vals, valid, sink_ref, write)
        return carry

    lax.fori_loop(0, WIN_SUB, block, 0, unroll=True)


def _window_attention(q, k, v, k_ctx, v_ctx, sink, nb, seq):
    n = q.shape[0]
    tq = WIN_SUB * WINDOW
    steps = seq // tq
    n_ctx = k_ctx.shape[1]
    qblk = pl.BlockSpec((tq, Q_B), lambda b, i: (b * steps + i, 0))
    kv = pl.BlockSpec((seq, K_B), lambda b, i: (b, 0))
    ctx = pl.BlockSpec((1, n_ctx, K_B), lambda b, i: (b, 0, 0))
    return pl.pallas_call(
        _win_attn_kernel,
        grid=(nb, steps),
        in_specs=[pl.BlockSpec(memory_space=pltpu.SMEM), qblk, kv, kv, ctx, ctx],
        out_specs=qblk,
        out_shape=jax.ShapeDtypeStruct((n, Q_B), F32),
        compiler_params=pltpu.CompilerParams(dimension_semantics=("parallel", "parallel")),
        name="window_attention",
    )(sink, q, k, v, k_ctx, v_ctx)


def _merge_kernel(x_ref, of_ref, ob_ref, yb_ref, sh_ref, sc_ref, g2_ref, nw_ref, on_ref,
                  wzg_ref, woa_ref, wob_ref, wout_ref, o_ref):
    x = x_ref[...]
    h = _modulated_norm(x, nw_ref, sh_ref, sc_ref)
    z = jnp.dot(h, wzg_ref[:, :Z_A], preferred_element_type=F32)
    o = of_ref[...] + ob_ref[...]
    heads = []
    for hh in range(H_A):
        sl = slice(hh * DV_A, (hh + 1) * DV_A)
        heads.append(_rms(o[:, sl], on_ref[...]) * _silu(z[:, sl]))
    y_a = jnp.concatenate(heads, axis=1)
    gate_a = jnp.dot(h, wzg_ref[:, Z_A:Z_A + D_MODEL], preferred_element_type=F32)
    gate_b = jnp.dot(h, wzg_ref[:, Z_A + D_MODEL:], preferred_element_type=F32)
    m = _sigmoid(gate_a) * _dot(y_a, woa_ref[...]) + _sigmoid(gate_b) * _dot(yb_ref[...], wob_ref[...])
    o_ref[...] = x + g2_ref[0, 0] * _dot(m, wout_ref[...])


def _merge(x, o_f, o_b, y_b, mod4, tokens_per_row, row0, norm_w, onorm, w_zg, w_oa, w_ob, w_out):
    n = x.shape[0]
    tm = MERGE_TILE
    tok = lambda w: pl.BlockSpec((tm, w), lambda i: (i, 0))
    return pl.pallas_call(
        _merge_kernel,
        grid=(n // tm,),
        in_specs=[tok(D_MODEL), tok(Z_A), tok(Z_A), tok(Q_B),
                  _mod_spec(3, tm, tokens_per_row, row0),
                  _mod_spec(4, tm, tokens_per_row, row0),
                  _mod_spec(5, tm, tokens_per_row, row0),
                  _const_spec((1, D_MODEL)),
                  _const_spec((1, DV_A)),
                  _const_spec((D_MODEL, Z_A + GATES)),
                  _const_spec((H_A * DV_A, D_MODEL)),
                  _const_spec((H_B * HD_B, D_MODEL)),
                  _const_spec((D_MODEL, D_MODEL))],
        out_specs=tok(D_MODEL),
        out_shape=jax.ShapeDtypeStruct((n, D_MODEL), F32),
        compiler_params=pltpu.CompilerParams(dimension_semantics=("parallel",),
                                             vmem_limit_bytes=VMEM_BIG),
        name="merge",
    )(x, o_f, o_b, y_b, mod4, mod4, mod4, norm_w, onorm, w_zg, w_oa, w_ob, w_out)


def _rope_tables(seq):
    rows = seq // GRID_W
    row = jnp.repeat(jnp.arange(rows, dtype=F32), GRID_W)
    col = jnp.tile(jnp.arange(GRID_W, dtype=F32), rows)
    inv = jnp.power(ROPE_BASE, -jnp.arange(ROPE_AXIS_PAIRS, dtype=F32) / ROPE_AXIS_PAIRS)
    ang = jnp.concatenate([row[:, None] * inv, col[:, None] * inv], axis=-1)
    cos, sin = jnp.cos(ang), jnp.sin(ang)
    cos_t = jnp.tile(cos, (1, LANES // (HD_B // 2)))
    sin_t = jnp.tile(jnp.concatenate([-sin, sin], axis=-1), (1, LANES // HD_B))
    return cos_t, sin_t


def _split_w_in(w_in):
    edges = np.cumsum((0, QKV_A, Z_A, DEC_A, BETA_A, Q_B, K_B, V_B, GATES))
    part = lambda i: w_in[:, int(edges[i]):int(edges[i + 1])].astype(BF)
    pad = jnp.zeros((D_MODEL, LANES - DEC_A - BETA_A), BF)
    w_mix = jnp.concatenate([part(0), part(4), part(5), part(6), part(2), part(3), pad], axis=1)
    w_zg = jnp.concatenate([part(1), part(7)], axis=1)
    return w_mix, w_zg


def _cache_layout(t, nb, seq):
    return t.reshape(nb, KV_B, HD_B, seq).transpose(0, 3, 1, 2)[:, None]


def kernel(x_prompt, x_sample, state_delta_fwd, state_delta_bwd, cache_k, cache_v, c, c_ctx, ada_w, ada_b, norm_ffn1, ffn1_w13, ffn1_w2, norm_mix, w_in, conv_w, a_log, dt_bias, onorm_a, w_oa, w_ob, w_out, sink, norm_ffn2, ffn2_w13, ffn2_w2, norm_final):
    bp, sp = x_prompt.shape[:2]
    bs, ts = x_sample.shape[:2]
    layer = 0

    cvec = jnp.concatenate([c_ctx[None, :], c, jnp.zeros((MOD_ROWS - 1 - bs, D_MODEL), F32)], axis=0)
    mod4 = _modulation(cvec, ada_w[layer], ada_b[layer]).reshape(MOD_ROWS, N_MOD, 1, D_MODEL)

    row = lambda a: a.reshape(1, -1)
    lane_pad = lambda a: jnp.pad(a.reshape(1, -1), ((0, 0), (0, LANES - a.size)))
    w13_1, w2_1 = ffn1_w13[layer].astype(BF), ffn1_w2[layer].astype(BF)
    w13_2, w2_2 = ffn2_w13[layer].astype(BF), ffn2_w2[layer].astype(BF)
    w_mix, w_zg = _split_w_in(w_in[layer])
    w_oa_b, w_ob_b, w_out_b = w_oa[layer].astype(BF), w_ob[layer].astype(BF), w_out[layer].astype(BF)
    alog_p, dtb_p = lane_pad(a_log[layer]), lane_pad(dt_bias[layer])
    cos_t, sin_t = _rope_tables(ts)
    final_w = row(norm_final)

    def run(x, nb, seq, tokens_per_row, row0, init, ctx):
        latent = ctx is not None
        x1 = _ffn(x, mod4, (0, 1, 2), tokens_per_row, row0, row(norm_ffn1[layer]), w13_1, w2_1,
                  final_w, final=False)
        mix = _inproj(x1, mod4, tokens_per_row, row0, row(norm_mix[layer]), w_mix, cos_t, sin_t,
                      conv_w[layer], alog_p, dtb_p, seq, latent)
        qkvc, qb, kb, vb, gbeta = mix[:5]
        scan = _delta_scan(qkvc, gbeta, init, nb, seq, emit_state=not latent)
        if latent:
            y_b = _window_attention(qb, kb, vb, ctx[0], ctx[1], sink[layer], nb, seq)
        else:
            y_b = _context_attention(qb, kb, vb, sink[layer], nb, seq)
        x2 = _merge(x1, scan[0], scan[1], y_b, mod4, tokens_per_row, row0, row(norm_mix[layer]),
                    row(onorm_a[layer]), w_zg, w_oa_b, w_ob_b, w_out_b)
        y = _ffn(x2, mod4, (6, 7, 8), tokens_per_row, row0, row(norm_ffn2[layer]), w13_2, w2_2,
                 final_w, final=True)
        return y, scan[2:], mix[5:]

    yp, (s_f, s_b), (k_t, v_t) = run(x_prompt.reshape(bp * sp, D_MODEL), bp, sp, bp * sp, 0, None, None)
    ctx = (cache_k[:, layer].reshape(bs, -1, K_B), cache_v[:, layer].reshape(bs, -1, V_B))
    ys, _, _ = run(x_sample.reshape(bs * ts, D_MODEL), bs, ts, ts, 1,
                   (state_delta_fwd[:, layer], state_delta_bwd[:, layer]), ctx)

    return (yp.reshape(bp, sp, D_MODEL), ys.reshape(bs, ts, D_MODEL),
            s_f[:, None], s_b[:, None], _cache_layout(k_t, bp, sp), _cache_layout(v_t, bp, sp))
```

```python
import functools

import jax
import jax.numpy as jnp
import numpy as np
from jax import lax
from jax.experimental import pallas as pl
from jax.experimental.pallas import tpu as pltpu

D_MODEL = 1024
GRID_W = 64
H_A = 4
DK_A = 128
DV_A = 128
CONV_W = 5
H_B = 8
KV_B = 2
HD_B = 64
WINDOW = 128
ROPE_BASE = 10000.0
ROPE_AXIS_PAIRS = HD_B // 4
D_FF = 2816
HALF_STEP = 0.5
N_MOD = 9
EPS = 1e-6

QKV_A = 2 * H_A * DK_A + H_A * DV_A
Z_A = H_A * DV_A
DEC_A = 2 * H_A
BETA_A = 2 * H_A
Q_B = H_B * HD_B
K_B = KV_B * HD_B
V_B = KV_B * HD_B
GATES = 2 * D_MODEL

LANES = 128
SUBLANES = 8
V7X_VMEM_BYTES = 64 * 1024 * 1024
NEG = -1e30

BF = jnp.bfloat16
F32 = jnp.float32

FFN_TILE = 1024
FFN_CHUNK = 256
IN_TILE_MAX = 1024
MERGE_TILE = 512
MOD_COLS = N_MOD * D_MODEL // 8
MOD_ROWS = 16
CHUNK = 128
SCAN_CHUNKS_PER_STEP = 4
WIN_SUB = 8
VMEM_BIG = V7X_VMEM_BYTES // 8 * 7


def _dot(a, b):
    return jnp.dot(a.astype(BF), b.astype(BF), preferred_element_type=F32)


def _dot_nt(a, b):
    return lax.dot_general(a.astype(BF), b.astype(BF), (((1,), (1,)), ((), ())),
                           preferred_element_type=F32)


def _sigmoid(x):
    return 0.5 + 0.5 * jnp.tanh(0.5 * x)


def _silu(x):
    half = 0.5 * x
    return half + half * jnp.tanh(half)


def _rms(x, w):
    return x * lax.rsqrt(jnp.mean(x * x, axis=-1, keepdims=True) + EPS) * w


def _modulated_norm(x, nw_ref, sh_ref, sc_ref):
    gain = nw_ref[...] * (1.0 + sc_ref[0, 0])
    xn = x * lax.rsqrt(jnp.mean(x * x, axis=-1, keepdims=True) + EPS)
    return (xn * gain + sh_ref[0, 0]).astype(BF)


def _const_spec(shape):
    nd = len(shape)
    return pl.BlockSpec(shape, lambda *_: (0,) * nd, pipeline_mode=pl.Buffered(1))


def _mod_spec(kind, tm, tokens_per_row, row0):
    return pl.BlockSpec((1, 1, 1, D_MODEL),
                        lambda i: (row0 + (i * tm) // tokens_per_row, kind, 0, 0))


def _mod_kernel(c_ref, w_ref, b_ref, o_ref):
    o_ref[...] = _dot(_silu(c_ref[...]), w_ref[...]) + b_ref[...]


def _modulation(cvec, ada_w, ada_b):
    rows = cvec.shape[0]
    n = N_MOD * D_MODEL
    bn = MOD_COLS
    return pl.pallas_call(
        _mod_kernel,
        grid=(n // bn,),
        in_specs=[pl.BlockSpec((rows, D_MODEL), lambda j: (0, 0)),
                  pl.BlockSpec((D_MODEL, bn), lambda j: (0, j)),
                  pl.BlockSpec((1, bn), lambda j: (0, j))],
        out_specs=pl.BlockSpec((rows, bn), lambda j: (0, j)),
        out_shape=jax.ShapeDtypeStruct((rows, n), F32),
        name="modulation",
    )(cvec, ada_w, ada_b.reshape(1, n))


def _ffn_kernel(x_ref, sh_ref, sc_ref, g_ref, nw_ref, w13_ref, w2_ref, fw_ref, o_ref, *, ff_chunk, final):
    x = x_ref[...]
    h = _modulated_norm(x, nw_ref, sh_ref, sc_ref)
    acc = jnp.zeros(x.shape, F32)
    for j in range(D_FF // ff_chunk):
        a = jnp.dot(h, w13_ref[:, j * ff_chunk:(j + 1) * ff_chunk], preferred_element_type=F32)
        b = jnp.dot(h, w13_ref[:, D_FF + j * ff_chunk:D_FF + (j + 1) * ff_chunk],
                    preferred_element_type=F32)
        t = (_silu(a) * b).astype(BF)
        acc = acc + jnp.dot(t, w2_ref[j * ff_chunk:(j + 1) * ff_chunk, :], preferred_element_type=F32)
    y = x + (HALF_STEP * g_ref[0, 0]) * acc
    if final:
        y = _rms(y, fw_ref[...])
    o_ref[...] = y


def _ffn(x, mod4, kinds, tokens_per_row, row0, norm_w, w13, w2, final_w, final):
    n = x.shape[0]
    tm = FFN_TILE
    tok = pl.BlockSpec((tm, D_MODEL), lambda i: (i, 0))
    return pl.pallas_call(
        functools.partial(_ffn_kernel, ff_chunk=FFN_CHUNK, final=final),
        grid=(n // tm,),
        in_specs=[tok,
                  _mod_spec(kinds[0], tm, tokens_per_row, row0),
                  _mod_spec(kinds[1], tm, tokens_per_row, row0),
                  _mod_spec(kinds[2], tm, tokens_per_row, row0),
                  _const_spec((1, D_MODEL)),
                  _const_spec((D_MODEL, 2 * D_FF)),
                  _const_spec((D_FF, D_MODEL)),
                  _const_spec((1, D_MODEL))],
        out_specs=tok,
        out_shape=jax.ShapeDtypeStruct((n, D_MODEL), F32),
        compiler_params=pltpu.CompilerParams(dimension_semantics=("parallel",),
                                             vmem_limit_bytes=VMEM_BIG),
        name="ffn_final" if final else "ffn",
    )(x, mod4, mod4, mod4, norm_w, w13, w2, final_w)


IN_WIDTH_P = QKV_A + Q_B + K_B + V_B + LANES
HALO = SUBLANES


def _swap_halves(x):
    w = x.shape[-1]
    lane = lax.broadcasted_iota(jnp.int32, x.shape, 1)
    lower = (lane % HD_B) < (HD_B // 2)
    return jnp.where(lower, pltpu.roll(x, w - HD_B // 2, 1), pltpu.roll(x, HD_B // 2, 1))


def _rope(p, cos_ref, sin_ref):
    reps = p.shape[1] // LANES
    c = jnp.concatenate([cos_ref[...]] * reps, axis=1) if reps > 1 else cos_ref[...]
    s = jnp.concatenate([sin_ref[...]] * reps, axis=1) if reps > 1 else sin_ref[...]
    return p * c + _swap_halves(p) * s


def _inproj_kernel(xm_ref, xp_ref, xn_ref, sh_ref, sc_ref, nw_ref, w_ref, cos_ref, sin_ref,
                   cw_ref, alog_ref, dtb_ref, qkv_ref, qb_ref, kb_ref, vb_ref, gbeta_ref, *rest,
                   latent, tiles):
    t = pl.program_id(0) % tiles
    x = jnp.concatenate([xp_ref[...], xm_ref[...], xn_ref[...]], axis=0)
    h = _modulated_norm(x, nw_ref, sh_ref, sc_ref)
    rows = x.shape[0]
    tile = rows - 2 * HALO
    blocks = rows // SUBLANES
    hm = h[HALO:HALO + tile]

    def project(first, width, lhs=hm):
        return jnp.dot(lhs, w_ref[:, first:first + width], preferred_element_type=F32)

    def conv_section(raw, first):
        width = raw.shape[1]
        r3 = raw.reshape(blocks, SUBLANES, width)
        r3 = jnp.concatenate([jnp.where(t == 0, 0.0, r3[0:1]), r3[1:blocks - 1],
                              jnp.where(t == tiles - 1, 0.0, r3[blocks - 1:blocks])], axis=0)
        sub = lax.broadcasted_iota(jnp.int32, (1, SUBLANES, width), 1)
        acc = None
        for j in range(CONV_W):
            s = CONV_W // 2 - j
            tap = cw_ref[j:j + 1, first:first + width].reshape(1, 1, width)
            if s == 0:
                sh = r3[1:blocks - 1]
            else:
                if s > 0:
                    m = jnp.where(sub < SUBLANES - s, r3[1:blocks - 1], r3[0:blocks - 2])
                else:
                    m = jnp.where(sub >= -s, r3[1:blocks - 1], r3[2:blocks])
                sh = pltpu.roll(m, s % SUBLANES, 1)
            acc = sh * tap if acc is None else acc + sh * tap
        y = _silu(acc.reshape(tile, width))
        for hh in range(width // DK_A):
            v = y[:, hh * DK_A:(hh + 1) * DK_A]
            if first < 2 * H_A * DK_A:
                scale = DK_A ** -0.5 if first < H_A * DK_A else 1.0
                v = v * (lax.rsqrt(jnp.sum(v * v, axis=-1, keepdims=True) + EPS) * scale)
            qkv_ref[:, first + hh * DK_A:first + (hh + 1) * DK_A] = v

    sec = H_A * DK_A
    qb0 = QKV_A
    kv0 = qb0 + Q_B
    db0 = kv0 + K_B + V_B
    for i in range(QKV_A // sec):
        conv_section(project(i * sec, sec, h), i * sec)
    qb = project(qb0, Q_B)
    qb_ref[...] = _rope(qb, cos_ref, sin_ref) if latent else qb
    kv = project(kv0, K_B + V_B)
    kb_ref[...] = _rope(kv[:, :K_B], cos_ref, sin_ref) if latent else kv[:, :K_B]
    vb_ref[...] = kv[:, K_B:]
    if not latent:
        kt_ref, vt_ref = rest
        kt_ref[0] = kv[:, :K_B].T
        vt_ref[0] = kv[:, K_B:].T
    db = project(db0, LANES)

    xg = db + dtb_ref[...]
    softplus = jnp.maximum(xg, 0.0) + jnp.log1p(jnp.exp(-jnp.abs(xg)))
    g = -jnp.exp(alog_ref[...]) * softplus
    lane = lax.broadcasted_iota(jnp.int32, db.shape, 1)
    gbeta_ref[...] = jnp.where(lane < DEC_A, g, jnp.where(lane < DEC_A + BETA_A, _sigmoid(db), 0.0))


def _inproj(x, mod4, tokens_per_row, row0, norm_w, w_mix, cos_t, sin_t, conv_w, alog_p, dtb_p, seq, latent):
    n = x.shape[0]
    tm = min(IN_TILE_MAX, seq)
    tiles = seq // tm
    halos_per_tile = tm // HALO
    last_halo = n // HALO - 1
    tok = lambda w: pl.BlockSpec((tm, w), lambda i: (i, 0))
    prev = pl.BlockSpec((HALO, D_MODEL), lambda i: (jnp.maximum(i * halos_per_tile - 1, 0), 0))
    nxt = pl.BlockSpec((HALO, D_MODEL), lambda i: (jnp.minimum((i + 1) * halos_per_tile, last_halo), 0))
    tab = pl.BlockSpec((tm, LANES), lambda i: (i % tiles, 0))
    widths = (QKV_A, Q_B, K_B, V_B, LANES)
    out_specs = [tok(w) for w in widths]
    out_shape = [jax.ShapeDtypeStruct((n, w), F32) for w in widths]
    if not latent:
        assert tiles == 1
        out_specs += [pl.BlockSpec((1, K_B, tm), lambda i: (i, 0, 0))] * 2
        out_shape += [jax.ShapeDtypeStruct((n // tm, K_B, tm), F32)] * 2
    return pl.pallas_call(
        functools.partial(_inproj_kernel, latent=latent, tiles=tiles),
        grid=(n // tm,),
        in_specs=[tok(D_MODEL), prev, nxt,
                  _mod_spec(3, tm, tokens_per_row, row0),
                  _mod_spec(4, tm, tokens_per_row, row0),
                  _const_spec((1, D_MODEL)),
                  _const_spec((D_MODEL, IN_WIDTH_P)),
                  tab, tab,
                  _const_spec((CONV_W, QKV_A)),
                  _const_spec((1, LANES)),
                  _const_spec((1, LANES))],
        out_specs=out_specs,
        out_shape=out_shape,
        compiler_params=pltpu.CompilerParams(dimension_semantics=("parallel",),
                                             vmem_limit_bytes=VMEM_BIG),
        name="inproj_latent" if latent else "inproj_context",
    )(x, x, x, mod4, mod4, norm_w, w_mix, cos_t, sin_t, conv_w, alog_p, dtb_p)


TRI_BASE = 16


def _unit_tri_inverses(lows):
    n = lows[0].shape[0]
    row = lax.broadcasted_iota(jnp.int32, (n, n), 0)
    col = lax.broadcasted_iota(jnp.int32, (n, n), 1)
    base = row // TRI_BASE == col // TRI_BASE
    eye = jnp.where(row == col, 1.0, 0.0)
    ps = [-jnp.where(base, low, 0.0) for low in lows]
    ts = [eye + p for p in ps]
    ps = [_dot(p, p) for p in ps]
    for _ in range(int(np.log2(TRI_BASE)) - 2):
        both = [_dot(jnp.concatenate([p.astype(BF), t.astype(BF)], axis=0), p) for p, t in zip(ps, ts)]
        ts = [t + b[n:] for t, b in zip(ts, both)]
        ps = [b[:n] for b in both]
    ts = [t + _dot(t, p) for t, p in zip(ts, ps)]
    size = TRI_BASE
    while size < n:
        pair = (row // (2 * size) == col // (2 * size)) & (row // size != col // size)
        mids = [_dot(jnp.where(pair, low, 0.0), t) for low, t in zip(lows, ts)]
        ts = [t - _dot(t, m) for t, m in zip(ts, mids)]
        size *= 2
    return ts


def _bf16_terms(x):
    hi = x.astype(BF)
    rest = x - hi.astype(F32)
    mid = rest.astype(BF)
    lo = (rest - mid.astype(F32)).astype(BF)
    return hi, mid, lo


def _scan_kernel(*refs, steps, chunks, has_init, emit_state):
    xf_ref, xb_ref, gf_ref, gb_ref = refs[:4]
    rest = list(refs[4:])
    s0_refs = (rest.pop(0), rest.pop(0)) if has_init else None
    of_ref, ob_ref = rest.pop(0), rest.pop(0)
    out_state_refs = (rest.pop(0), rest.pop(0)) if emit_state else None
    state, = rest
    c = pl.program_id(1)

    @pl.when(c == 0)
    def _():
        if has_init:
            state[0:H_A] = s0_refs[0][0]
            state[H_A:2 * H_A] = s0_refs[1][0]
        else:
            state[...] = jnp.zeros(state.shape, F32)

    row = lax.broadcasted_iota(jnp.int32, (CHUNK, CHUNK), 0)
    col = lax.broadcasted_iota(jnp.int32, (CHUNK, CHUNK), 1)
    x_refs, g_refs, o_refs = (xf_ref, xb_ref), (gf_ref, gb_ref), (of_ref, ob_ref)
    incl = (row >= col, row <= col)
    strict = (row > col, row < col)
    tri = [jnp.where(m, 1.0, 0.0).astype(BF) for m in incl]
    chains = [(d, hh) for d in range(2) for hh in range(H_A)]
    nch = len(chains)

    def one_chunk(sub, carry):
        rows = (pl.ds(pl.multiple_of(sub * CHUNK, CHUNK), CHUNK),
                pl.ds(pl.multiple_of((chunks - 1 - sub) * CHUNK, CHUNK), CHUNK))
        gbeta = [g_refs[d][rows[d], :] for d in range(2)]
        gc = [sum(jnp.dot(tri[d], part, preferred_element_type=F32) for part in _bf16_terms(gbeta[d]))
              for d in range(2)]
        gct = [g.T for g in gc]
        q = [x_refs[d][rows[d], hh * DK_A:(hh + 1) * DK_A] for d, hh in chains]
        k = [x_refs[d][rows[d], (H_A + hh) * DK_A:(H_A + hh + 1) * DK_A] for d, hh in chains]
        v = [x_refs[d][rows[d], (2 * H_A + hh) * DK_A:(2 * H_A + hh + 1) * DK_A] for d, hh in chains]
        gcol = [gc[d][:, d * H_A + hh:d * H_A + hh + 1] for d, hh in chains]
        beta = [gbeta[d][:, DEC_A + d * H_A + hh:DEC_A + d * H_A + hh + 1] for d, hh in chains]
        decay = [jnp.exp(jnp.where(incl[d], gcol[i] - gct[d][d * H_A + hh:d * H_A + hh + 1, :], NEG))
                 for i, (d, hh) in enumerate(chains)]
        qk_kk = [_dot_nt(jnp.concatenate([q[i].astype(BF), k[i].astype(BF)], axis=0), k[i]) for i in range(nch)]
        tmat = _unit_tri_inverses([jnp.where(strict[d], qk_kk[i][CHUNK:] * beta[i] * decay[i], 0.0)
                                   for i, (d, hh) in enumerate(chains)])
        egc = [jnp.exp(g) for g in gcol]
        uw = [_dot(tmat[i], jnp.concatenate([v[i] * beta[i], k[i] * (beta[i] * egc[i])], axis=1))
              for i in range(nch)]
        g_last = [gcol[i][CHUNK - 1:CHUNK] if d == 0 else gcol[i][0:1] for i, (d, hh) in enumerate(chains)]
        s_old = [state[i] for i in range(nch)]
        s_bf = [s.astype(BF) for s in s_old]
        ws_qs = [_dot(jnp.concatenate([uw[i][:, DV_A:].astype(BF), (q[i] * egc[i]).astype(BF)], axis=0), s_bf[i])
                 for i in range(nch)]
        v_new = [uw[i][:, :DV_A] - ws_qs[i][:CHUNK] for i in range(nch)]
        kd_t = [(k[i] * jnp.exp(g_last[i] - gcol[i])).T for i in range(nch)]
        av_kv = [_dot(jnp.concatenate([(qk_kk[i][:CHUNK] * decay[i]).astype(BF), kd_t[i].astype(BF)], axis=0),
                      v_new[i]) for i in range(nch)]
        for i, (d, hh) in enumerate(chains):
            o_refs[d][rows[d], hh * DV_A:(hh + 1) * DV_A] = ws_qs[i][CHUNK:] + av_kv[i][:CHUNK]
        for i in range(nch):
            state[i] = s_old[i] * jnp.exp(g_last[i]) + av_kv[i][CHUNK:]
        return carry

    lax.fori_loop(0, chunks, one_chunk, 0, unroll=True)

    if emit_state:
        @pl.when(c == steps - 1)
        def _():
            out_state_refs[0][0] = state[0:H_A]
            out_state_refs[1][0] = state[H_A:2 * H_A]


def _delta_scan(qkvc, gbeta, init, nb, seq, emit_state):
    n = qkvc.shape[0]
    chunks = min(SCAN_CHUNKS_PER_STEP, seq // CHUNK)
    rows = chunks * CHUNK
    steps = seq // rows
    fwd = lambda w: pl.BlockSpec((rows, w), lambda b, c: (b * steps + c, 0))
    bwd = lambda w: pl.BlockSpec((rows, w), lambda b, c: (b * steps + steps - 1 - c, 0))
    st = pl.BlockSpec((1, H_A, DK_A, DV_A), lambda b, c: (b, 0, 0, 0))
    st_shape = jax.ShapeDtypeStruct((nb, H_A, DK_A, DV_A), F32)
    o_shape = jax.ShapeDtypeStruct((n, Z_A), F32)
    has_init = init is not None
    return pl.pallas_call(
        functools.partial(_scan_kernel, steps=steps, chunks=chunks, has_init=has_init, emit_state=emit_state),
        grid=(nb, steps),
        in_specs=[fwd(QKV_A), bwd(QKV_A), fwd(LANES), bwd(LANES)] + ([st, st] if has_init else []),
        out_specs=[fwd(Z_A), bwd(Z_A)] + ([st, st] if emit_state else []),
        out_shape=[o_shape, o_shape] + ([st_shape, st_shape] if emit_state else []),
        scratch_shapes=[pltpu.VMEM((2 * H_A, DK_A, DV_A), F32)],
        compiler_params=pltpu.CompilerParams(dimension_semantics=("parallel", "arbitrary")),
        name="delta_scan",
    )(qkvc, qkvc, gbeta, gbeta, *(init if has_init else ()))


LOG2E = float(np.log2(np.e))
ATTN_SCALE = HD_B ** -0.5 * LOG2E


def _dup_keys(keys):
    klane = lax.broadcasted_iota(jnp.int32, keys.shape, 1)
    krot = pltpu.roll(keys, HD_B, 1)
    return [jnp.where(klane < HD_B, keys, krot).astype(BF), jnp.where(klane < HD_B, krot, keys).astype(BF)]


def _attn_core(q, kdup, v_bf, valid, sink_ref, write):
    nq = q.shape[0]
    group = H_B // KV_B
    qlow = lax.broadcasted_iota(jnp.int32, (nq, LANES), 1) < HD_B
    scores = []
    for head in range(H_B):
        qt = q[:, (head // 2) * LANES:(head // 2 + 1) * LANES]
        qh = jnp.where(qlow, qt, 0.0) if head % 2 == 0 else jnp.where(qlow, 0.0, qt)
        scores.append(_dot_nt(qh, kdup[head // group]))
    probs, inv_den = [], []
    for head, s in enumerate(scores):
        if valid is not None:
            s = jnp.where(valid, s, NEG)
        snk = sink_ref[head] * LOG2E
        m = jnp.maximum(jnp.max(s, axis=-1, keepdims=True), snk)
        p = jnp.exp2(s - m)
        inv_den.append(1.0 / (jnp.sum(p, axis=-1, keepdims=True) + jnp.exp2(snk - m)))
        probs.append(p.astype(BF))
    o_all = [jnp.dot(jnp.concatenate(probs[g * group:(g + 1) * group], axis=0), v_bf,
                     preferred_element_type=F32) for g in range(KV_B)]
    for g in range(KV_B):
        for t in range(group // 2):
            a = o_all[g][(2 * t) * nq:(2 * t + 1) * nq] * inv_den[g * group + 2 * t]
            b = o_all[g][(2 * t + 1) * nq:(2 * t + 2) * nq] * inv_den[g * group + 2 * t + 1]
            if g == 0:
                tile = jnp.where(qlow, a, pltpu.roll(b, HD_B, 1))
            else:
                tile = jnp.where(qlow, pltpu.roll(a, HD_B, 1), b)
            write((g * group // 2 + t) * LANES, tile)


def _ctx_attn_kernel(sink_ref, q_ref, k_ref, v_ref, o_ref):
    def write(col, tile):
        o_ref[:, col:col + LANES] = tile

    _attn_core(q_ref[...] * ATTN_SCALE, _dup_keys(k_ref[...]), v_ref[...].astype(BF), None, sink_ref, write)


def _context_attention(q, k, v, sink, nb, seq):
    n = q.shape[0]
    blk = lambda w: pl.BlockSpec((seq, w), lambda b: (b, 0))
    return pl.pallas_call(
        _ctx_attn_kernel,
        grid=(nb,),
        in_specs=[pl.BlockSpec(memory_space=pltpu.SMEM), blk(Q_B), blk(K_B), blk(V_B)],
        out_specs=blk(Q_B),
        out_shape=jax.ShapeDtypeStruct((n, Q_B), F32),
        compiler_params=pltpu.CompilerParams(dimension_semantics=("parallel",)),
        name="context_attention",
    )(sink, q, k, v)


def _win_attn_kernel(sink_ref, q_ref, k_ref, v_ref, kx_ref, vx_ref, o_ref, kd_ref, vb_ref, kxd_ref, vxb_ref):
    step = pl.program_id(1)
    seq = k_ref.shape[0]
    n_local = 3 * WINDOW
    nk = n_local + kx_ref.shape[1]
    r = lax.broadcasted_iota(jnp.int32, (WINDOW, nk), 0)
    j = lax.broadcasted_iota(jnp.int32, (WINDOW, nk), 1)

    @pl.when(step == 0)
    def _():
        for g, kd in enumerate(_dup_keys(k_ref[...])):
            kd_ref[g] = kd
        for g, kd in enumerate(_dup_keys(kx_ref[0])):
            kxd_ref[g] = kd
        vb_ref[...] = v_ref[...].astype(BF)
        vxb_ref[...] = vx_ref[0].astype(BF)

    def block(sub, carry):
        i = step * WIN_SUB + sub
        start = pl.multiple_of(jnp.clip((i - 1) * WINDOW, 0, seq - n_local), WINDOW)
        kdup = [jnp.concatenate([kd_ref[g, pl.ds(start, n_local), :], kxd_ref[g]], axis=0) for g in range(KV_B)]
        v_bf = jnp.concatenate([vb_ref[pl.ds(start, n_local), :], vxb_ref[...]], axis=0)
        dist = (i * WINDOW - start) + r - j
        valid = ((dist <= WINDOW) & (dist >= -WINDOW)) | (j >= n_local)
        rows = pl.ds(pl.multiple_of(sub * WINDOW, WINDOW), WINDOW)

        def write(col, tile):
            o_ref[rows, col:col + LANES] = tile

        _attn_core(q_ref[rows, :] * ATTN_SCALE, kdup, v_bf, valid, sink_ref, write)
        return carry

    lax.fori_loop(0, WIN_SUB, block, 0, unroll=True)


def _window_attention(q, k, v, k_ctx, v_ctx, sink, nb, seq):
    n = q.shape[0]
    tq = WIN_SUB * WINDOW
    steps = seq // tq
    n_ctx = k_ctx.shape[1]
    qblk = pl.BlockSpec((tq, Q_B), lambda b, i: (b * steps + i, 0))
    kv = pl.BlockSpec((seq, K_B), lambda b, i: (b, 0))
    ctx = pl.BlockSpec((1, n_ctx, K_B), lambda b, i: (b, 0, 0))
    return pl.pallas_call(
        _win_attn_kernel,
        grid=(nb, steps),
        in_specs=[pl.BlockSpec(memory_space=pltpu.SMEM), qblk, kv, kv, ctx, ctx],
        out_specs=qblk,
        out_shape=jax.ShapeDtypeStruct((n, Q_B), F32),
        scratch_shapes=[pltpu.VMEM((KV_B, seq, K_B), BF), pltpu.VMEM((seq, V_B), BF),
                        pltpu.VMEM((KV_B, n_ctx, K_B), BF), pltpu.VMEM((n_ctx, V_B), BF)],
        compiler_params=pltpu.CompilerParams(dimension_semantics=("parallel", "arbitrary")),
        name="window_attention",
    )(sink, q, k, v, k_ctx, v_ctx)


def _merge_kernel(x_ref, of_ref, ob_ref, yb_ref, sh_ref, sc_ref, g2_ref, nw_ref, on_ref,
                  wzg_ref, woa_ref, wob_ref, wout_ref, o_ref):
    x = x_ref[...]
    h = _modulated_norm(x, nw_ref, sh_ref, sc_ref)
    z = jnp.dot(h, wzg_ref[:, :Z_A], preferred_element_type=F32)
    o = of_ref[...] + ob_ref[...]
    heads = []
    for hh in range(H_A):
        sl = slice(hh * DV_A, (hh + 1) * DV_A)
        heads.append(_rms(o[:, sl], on_ref[...]) * _silu(z[:, sl]))
    y_a = jnp.concatenate(heads, axis=1)
    gate_a = jnp.dot(h, wzg_ref[:, Z_A:Z_A + D_MODEL], preferred_element_type=F32)
    gate_b = jnp.dot(h, wzg_ref[:, Z_A + D_MODEL:], preferred_element_type=F32)
    m = _sigmoid(gate_a) * _dot(y_a, woa_ref[...]) + _sigmoid(gate_b) * _dot(yb_ref[...], wob_ref[...])
    o_ref[...] = x + g2_ref[0, 0] * _dot(m, wout_ref[...])


def _merge(x, o_f, o_b, y_b, mod4, tokens_per_row, row0, norm_w, onorm, w_zg, w_oa, w_ob, w_out):
    n = x.shape[0]
    tm = MERGE_TILE
    tok = lambda w: pl.BlockSpec((tm, w), lambda i: (i, 0))
    return pl.pallas_call(
        _merge_kernel,
        grid=(n // tm,),
        in_specs=[tok(D_MODEL), tok(Z_A), tok(Z_A), tok(Q_B),
                  _mod_spec(3, tm, tokens_per_row, row0),
                  _mod_spec(4, tm, tokens_per_row, row0),
                  _mod_spec(5, tm, tokens_per_row, row0),
                  _const_spec((1, D_MODEL)),
                  _const_spec((1, DV_A)),
                  _const_spec((D_MODEL, Z_A + GATES)),
                  _const_spec((H_A * DV_A, D_MODEL)),
                  _const_spec((H_B * HD_B, D_MODEL)),
                  _const_spec((D_MODEL, D_MODEL))],
        out_specs=tok(D_MODEL),
        out_shape=jax.ShapeDtypeStruct((n, D_MODEL), F32),
        compiler_params=pltpu.CompilerParams(dimension_semantics=("parallel",),
                                             vmem_limit_bytes=VMEM_BIG),
        name="merge",
    )(x, o_f, o_b, y_b, mod4, mod4, mod4, norm_w, onorm, w_zg, w_oa, w_ob, w_out)


def _rope_tables(seq):
    rows = seq // GRID_W
    row = jnp.repeat(jnp.arange(rows, dtype=F32), GRID_W)
    col = jnp.tile(jnp.arange(GRID_W, dtype=F32), rows)
    inv = jnp.power(ROPE_BASE, -jnp.arange(ROPE_AXIS_PAIRS, dtype=F32) / ROPE_AXIS_PAIRS)
    ang = jnp.concatenate([row[:, None] * inv, col[:, None] * inv], axis=-1)
    cos, sin = jnp.cos(ang), jnp.sin(ang)
    cos_t = jnp.tile(cos, (1, LANES // (HD_B // 2)))
    sin_t = jnp.tile(jnp.concatenate([-sin, sin], axis=-1), (1, LANES // HD_B))
    return cos_t, sin_t


def _split_w_in(w_in):
    edges = np.cumsum((0, QKV_A, Z_A, DEC_A, BETA_A, Q_B, K_B, V_B, GATES))
    part = lambda i: w_in[:, int(edges[i]):int(edges[i + 1])].astype(BF)
    pad = jnp.zeros((D_MODEL, LANES - DEC_A - BETA_A), BF)
    w_mix = jnp.concatenate([part(0), part(4), part(5), part(6), part(2), part(3), pad], axis=1)
    w_zg = jnp.concatenate([part(1), part(7)], axis=1)
    return w_mix, w_zg


def _cache_layout(t, nb, seq):
    return t.reshape(nb, KV_B, HD_B, seq).transpose(0, 3, 1, 2)[:, None]


def kernel(x_prompt, x_sample, state_delta_fwd, state_delta_bwd, cache_k, cache_v, c, c_ctx, ada_w, ada_b, norm_ffn1, ffn1_w13, ffn1_w2, norm_mix, w_in, conv_w, a_log, dt_bias, onorm_a, w_oa, w_ob, w_out, sink, norm_ffn2, ffn2_w13, ffn2_w2, norm_final):
    bp, sp = x_prompt.shape[:2]
    bs, ts = x_sample.shape[:2]
    layer = 0

    cvec = jnp.concatenate([c_ctx[None, :], c, jnp.zeros((MOD_ROWS - 1 - bs, D_MODEL), F32)], axis=0)
    mod4 = _modulation(cvec, ada_w[layer], ada_b[layer]).reshape(MOD_ROWS, N_MOD, 1, D_MODEL)

    row = lambda a: a.reshape(1, -1)
    lane_pad = lambda a: jnp.pad(a.reshape(1, -1), ((0, 0), (0, LANES - a.size)))
    w13_1, w2_1 = ffn1_w13[layer].astype(BF), ffn1_w2[layer].astype(BF)
    w13_2, w2_2 = ffn2_w13[layer].astype(BF), ffn2_w2[layer].astype(BF)
    w_mix, w_zg = _split_w_in(w_in[layer])
    w_oa_b, w_ob_b, w_out_b = w_oa[layer].astype(BF), w_ob[layer].astype(BF), w_out[layer].astype(BF)
    alog_p, dtb_p = lane_pad(a_log[layer]), lane_pad(dt_bias[layer])
    cos_t, sin_t = _rope_tables(ts)
    final_w = row(norm_final)

    def run(x, nb, seq, tokens_per_row, row0, init, ctx):
        latent = ctx is not None
        x1 = _ffn(x, mod4, (0, 1, 2), tokens_per_row, row0, row(norm_ffn1[layer]), w13_1, w2_1,
                  final_w, final=False)
        mix = _inproj(x1, mod4, tokens_per_row, row0, row(norm_mix[layer]), w_mix, cos_t, sin_t,
                      conv_w[layer], alog_p, dtb_p, seq, latent)
        qkvc, qb, kb, vb, gbeta = mix[:5]
        scan = _delta_scan(qkvc, gbeta, init, nb, seq, emit_state=not latent)
        if latent:
            y_b = _window_attention(qb, kb, vb, ctx[0], ctx[1], sink[layer], nb, seq)
        else:
            y_b = _context_attention(qb, kb, vb, sink[layer], nb, seq)
        x2 = _merge(x1, scan[0], scan[1], y_b, mod4, tokens_per_row, row0, row(norm_mix[layer]),
                    row(onorm_a[layer]), w_zg, w_oa_b, w_ob_b, w_out_b)
        y = _ffn(x2, mod4, (6, 7, 8), tokens_per_row, row0, row(norm_ffn2[layer]), w13_2, w2_2,
                 final_w, final=True)
        return y, scan[2:], mix[5:]

    yp, (s_f, s_b), (k_t, v_t) = run(x_prompt.reshape(bp * sp, D_MODEL), bp, sp, bp * sp, 0, None, None)
    ctx = (cache_k[:, layer].reshape(bs, -1, K_B), cache_v[:, layer].reshape(bs, -1, V_B))
    ys, _, _ = run(x_sample.reshape(bs * ts, D_MODEL), bs, ts, ts, 1,
                   (state_delta_fwd[:, layer], state_delta_bwd[:, layer]), ctx)

    return (yp.reshape(bp, sp, D_MODEL), ys.reshape(bs, ts, D_MODEL),
            s_f[:, None], s_b[:, None], _cache_layout(k_t, bp, sp), _cache_layout(v_t, bp, sp))
```

```python
import functools

import jax
import jax.numpy as jnp
import numpy as np
from jax import lax
from jax.experimental import pallas as pl
from jax.experimental.pallas import tpu as pltpu

D_MODEL = 1024
GRID_W = 64
H_A = 4
DK_A = 128
DV_A = 128
CONV_W = 5
H_B = 8
KV_B = 2
HD_B = 64
WINDOW = 128
ROPE_BASE = 10000.0
ROPE_AXIS_PAIRS = HD_B // 4
D_FF = 2816
HALF_STEP = 0.5
N_MOD = 9
EPS = 1e-6

QKV_A = 2 * H_A * DK_A + H_A * DV_A
Z_A = H_A * DV_A
DEC_A = 2 * H_A
BETA_A = 2 * H_A
Q_B = H_B * HD_B
K_B = KV_B * HD_B
V_B = KV_B * HD_B
GATES = 2 * D_MODEL

LANES = 128
SUBLANES = 8
V7X_VMEM_BYTES = 64 * 1024 * 1024
NEG = -1e30

BF = jnp.bfloat16
F32 = jnp.float32

FFN_TILE = 1024
FFN_CHUNK = 256
IN_TILE_MAX = 1024
MERGE_TILE = 512
MOD_COLS = N_MOD * D_MODEL // 8
MOD_ROWS = 16
CHUNK = 128
SCAN_CHUNKS_PER_STEP = 4
WIN_SUB = 8
VMEM_BIG = V7X_VMEM_BYTES // 8 * 7


def _dot(a, b):
    return jnp.dot(a.astype(BF), b.astype(BF), preferred_element_type=F32)


def _dot_nt(a, b):
    return lax.dot_general(a.astype(BF), b.astype(BF), (((1,), (1,)), ((), ())),
                           preferred_element_type=F32)


def _sigmoid(x):
    return 0.5 + 0.5 * jnp.tanh(0.5 * x)


def _silu(x):
    half = 0.5 * x
    return half + half * jnp.tanh(half)


def _rms(x, w):
    return x * lax.rsqrt(jnp.mean(x * x, axis=-1, keepdims=True) + EPS) * w


def _modulated_norm(x, nw_ref, sh_ref, sc_ref):
    gain = nw_ref[...] * (1.0 + sc_ref[0, 0])
    xn = x * lax.rsqrt(jnp.mean(x * x, axis=-1, keepdims=True) + EPS)
    return (xn * gain + sh_ref[0, 0]).astype(BF)


def _const_spec(shape):
    nd = len(shape)
    return pl.BlockSpec(shape, lambda *_: (0,) * nd, pipeline_mode=pl.Buffered(1))


def _mod_spec(kind, tm, tokens_per_row, row0):
    return pl.BlockSpec((1, 1, 1, D_MODEL),
                        lambda i: (row0 + (i * tm) // tokens_per_row, kind, 0, 0))


def _mod_kernel(c_ref, w_ref, b_ref, o_ref):
    o_ref[...] = _dot(_silu(c_ref[...]), w_ref[...]) + b_ref[...]


def _modulation(cvec, ada_w, ada_b):
    rows = cvec.shape[0]
    n = N_MOD * D_MODEL
    bn = MOD_COLS
    return pl.pallas_call(
        _mod_kernel,
        grid=(n // bn,),
        in_specs=[pl.BlockSpec((rows, D_MODEL), lambda j: (0, 0)),
                  pl.BlockSpec((D_MODEL, bn), lambda j: (0, j)),
                  pl.BlockSpec((1, bn), lambda j: (0, j))],
        out_specs=pl.BlockSpec((rows, bn), lambda j: (0, j)),
        out_shape=jax.ShapeDtypeStruct((rows, n), F32),
        name="modulation",
    )(cvec, ada_w, ada_b.reshape(1, n))


def _ffn_kernel(x_ref, sh_ref, sc_ref, g_ref, nw_ref, w13_ref, w2_ref, fw_ref, o_ref, *, ff_chunk, final):
    x = x_ref[...]
    h = _modulated_norm(x, nw_ref, sh_ref, sc_ref)
    acc = jnp.zeros(x.shape, F32)
    for j in range(D_FF // ff_chunk):
        a = jnp.dot(h, w13_ref[:, j * ff_chunk:(j + 1) * ff_chunk], preferred_element_type=F32)
        b = jnp.dot(h, w13_ref[:, D_FF + j * ff_chunk:D_FF + (j + 1) * ff_chunk],
                    preferred_element_type=F32)
        t = (_silu(a) * b).astype(BF)
        acc = acc + jnp.dot(t, w2_ref[j * ff_chunk:(j + 1) * ff_chunk, :], preferred_element_type=F32)
    y = x + (HALF_STEP * g_ref[0, 0]) * acc
    if final:
        y = _rms(y, fw_ref[...])
    o_ref[...] = y


def _ffn(x, mod4, kinds, tokens_per_row, row0, norm_w, w13, w2, final_w, final):
    n = x.shape[0]
    tm = FFN_TILE
    tok = pl.BlockSpec((tm, D_MODEL), lambda i: (i, 0))
    return pl.pallas_call(
        functools.partial(_ffn_kernel, ff_chunk=FFN_CHUNK, final=final),
        grid=(n // tm,),
        in_specs=[tok,
                  _mod_spec(kinds[0], tm, tokens_per_row, row0),
                  _mod_spec(kinds[1], tm, tokens_per_row, row0),
                  _mod_spec(kinds[2], tm, tokens_per_row, row0),
                  _const_spec((1, D_MODEL)),
                  _const_spec((D_MODEL, 2 * D_FF)),
                  _const_spec((D_FF, D_MODEL)),
                  _const_spec((1, D_MODEL))],
        out_specs=tok,
        out_shape=jax.ShapeDtypeStruct((n, D_MODEL), F32),
        compiler_params=pltpu.CompilerParams(dimension_semantics=("parallel",),
                                             vmem_limit_bytes=VMEM_BIG),
        name="ffn_final" if final else "ffn",
    )(x, mod4, mod4, mod4, norm_w, w13, w2, final_w)


IN_WIDTH_P = QKV_A + Q_B + K_B + V_B + LANES
HALO = SUBLANES


def _swap_halves(x):
    w = x.shape[-1]
    lane = lax.broadcasted_iota(jnp.int32, x.shape, 1)
    lower = (lane % HD_B) < (HD_B // 2)
    return jnp.where(lower, pltpu.roll(x, w - HD_B // 2, 1), pltpu.roll(x, HD_B // 2, 1))


def _rope(p, cos_ref, sin_ref):
    reps = p.shape[1] // LANES
    c = jnp.concatenate([cos_ref[...]] * reps, axis=1) if reps > 1 else cos_ref[...]
    s = jnp.concatenate([sin_ref[...]] * reps, axis=1) if reps > 1 else sin_ref[...]
    return p * c + _swap_halves(p) * s


def _inproj_kernel(xm_ref, xp_ref, xn_ref, sh_ref, sc_ref, nw_ref, w_ref, cos_ref, sin_ref,
                   cw_ref, alog_ref, dtb_ref, qkv_ref, qb_ref, kb_ref, vb_ref, gbeta_ref, *rest,
                   latent, tiles):
    t = pl.program_id(0) % tiles
    x = jnp.concatenate([xp_ref[...], xm_ref[...], xn_ref[...]], axis=0)
    h = _modulated_norm(x, nw_ref, sh_ref, sc_ref)
    rows = x.shape[0]
    tile = rows - 2 * HALO
    blocks = rows // SUBLANES
    hm = h[HALO:HALO + tile]

    def project(first, width, lhs=hm):
        return jnp.dot(lhs, w_ref[:, first:first + width], preferred_element_type=F32)

    def conv_section(raw, first):
        width = raw.shape[1]
        r3 = raw.reshape(blocks, SUBLANES, width)
        r3 = jnp.concatenate([jnp.where(t == 0, 0.0, r3[0:1]), r3[1:blocks - 1],
                              jnp.where(t == tiles - 1, 0.0, r3[blocks - 1:blocks])], axis=0)
        sub = lax.broadcasted_iota(jnp.int32, (1, SUBLANES, width), 1)
        acc = None
        for j in range(CONV_W):
            s = CONV_W // 2 - j
            tap = cw_ref[j:j + 1, first:first + width].reshape(1, 1, width)
            if s == 0:
                sh = r3[1:blocks - 1]
            else:
                if s > 0:
                    m = jnp.where(sub < SUBLANES - s, r3[1:blocks - 1], r3[0:blocks - 2])
                else:
                    m = jnp.where(sub >= -s, r3[1:blocks - 1], r3[2:blocks])
                sh = pltpu.roll(m, s % SUBLANES, 1)
            acc = sh * tap if acc is None else acc + sh * tap
        y = _silu(acc.reshape(tile, width))
        for hh in range(width // DK_A):
            v = y[:, hh * DK_A:(hh + 1) * DK_A]
            if first < 2 * H_A * DK_A:
                scale = DK_A ** -0.5 if first < H_A * DK_A else 1.0
                v = v * (lax.rsqrt(jnp.sum(v * v, axis=-1, keepdims=True) + EPS) * scale)
            qkv_ref[:, first + hh * DK_A:first + (hh + 1) * DK_A] = v

    sec = H_A * DK_A
    qb0 = QKV_A
    kv0 = qb0 + Q_B
    db0 = kv0 + K_B + V_B
    for i in range(QKV_A // sec):
        conv_section(project(i * sec, sec, h), i * sec)
    qb = project(qb0, Q_B)
    qb_ref[...] = _rope(qb, cos_ref, sin_ref) if latent else qb
    kv = project(kv0, K_B + V_B)
    kb_ref[...] = _rope(kv[:, :K_B], cos_ref, sin_ref) if latent else kv[:, :K_B]
    vb_ref[...] = kv[:, K_B:]
    if not latent:
        kt_ref, vt_ref = rest
        kt_ref[0] = kv[:, :K_B].T
        vt_ref[0] = kv[:, K_B:].T
    db = project(db0, LANES)

    xg = db + dtb_ref[...]
    softplus = jnp.maximum(xg, 0.0) + jnp.log1p(jnp.exp(-jnp.abs(xg)))
    g = -jnp.exp(alog_ref[...]) * softplus
    lane = lax.broadcasted_iota(jnp.int32, db.shape, 1)
    gbeta_ref[...] = jnp.where(lane < DEC_A, g, jnp.where(lane < DEC_A + BETA_A, _sigmoid(db), 0.0))


def _inproj(x, mod4, tokens_per_row, row0, norm_w, w_mix, cos_t, sin_t, conv_w, alog_p, dtb_p, seq, latent):
    n = x.shape[0]
    tm = min(IN_TILE_MAX, seq)
    tiles = seq // tm
    halos_per_tile = tm // HALO
    last_halo = n // HALO - 1
    tok = lambda w: pl.BlockSpec((tm, w), lambda i: (i, 0))
    prev = pl.BlockSpec((HALO, D_MODEL), lambda i: (jnp.maximum(i * halos_per_tile - 1, 0), 0))
    nxt = pl.BlockSpec((HALO, D_MODEL), lambda i: (jnp.minimum((i + 1) * halos_per_tile, last_halo), 0))
    tab = pl.BlockSpec((tm, LANES), lambda i: (i % tiles, 0))
    widths = (QKV_A, Q_B, K_B, V_B, LANES)
    out_specs = [tok(w) for w in widths]
    out_shape = [jax.ShapeDtypeStruct((n, w), F32) for w in widths]
    if not latent:
        assert tiles == 1
        out_specs += [pl.BlockSpec((1, K_B, tm), lambda i: (i, 0, 0))] * 2
        out_shape += [jax.ShapeDtypeStruct((n // tm, K_B, tm), F32)] * 2
    return pl.pallas_call(
        functools.partial(_inproj_kernel, latent=latent, tiles=tiles),
        grid=(n // tm,),
        in_specs=[tok(D_MODEL), prev, nxt,
                  _mod_spec(3, tm, tokens_per_row, row0),
                  _mod_spec(4, tm, tokens_per_row, row0),
                  _const_spec((1, D_MODEL)),
                  _const_spec((D_MODEL, IN_WIDTH_P)),
                  tab, tab,
                  _const_spec((CONV_W, QKV_A)),
                  _const_spec((1, LANES)),
                  _const_spec((1, LANES))],
        out_specs=out_specs,
        out_shape=out_shape,
        compiler_params=pltpu.CompilerParams(dimension_semantics=("parallel",),
                                             vmem_limit_bytes=VMEM_BIG),
        name="inproj_latent" if latent else "inproj_context",
    )(x, x, x, mod4, mod4, norm_w, w_mix, cos_t, sin_t, conv_w, alog_p, dtb_p)


TRI_BASE = 16


def _unit_tri_inverses(lows):
    n = lows[0].shape[0]
    row = lax.broadcasted_iota(jnp.int32, (n, n), 0)
    col = lax.broadcasted_iota(jnp.int32, (n, n), 1)
    base = row // TRI_BASE == col // TRI_BASE
    eye = jnp.where(row == col, 1.0, 0.0)
    ps = [-jnp.where(base, low, 0.0) for low in lows]
    ts = [eye + p for p in ps]
    ps = [_dot(p, p) for p in ps]
    for _ in range(int(np.log2(TRI_BASE)) - 2):
        both = [_dot(jnp.concatenate([p.astype(BF), t.astype(BF)], axis=0), p) for p, t in zip(ps, ts)]
        ts = [t + b[n:] for t, b in zip(ts, both)]
        ps = [b[:n] for b in both]
    ts = [t + _dot(t, p) for t, p in zip(ts, ps)]
    size = TRI_BASE
    while size < n:
        pair = (row // (2 * size) == col // (2 * size)) & (row // size != col // size)
        mids = [_dot(jnp.where(pair, low, 0.0), t) for low, t in zip(lows, ts)]
        ts = [t - _dot(t, m) for t, m in zip(ts, mids)]
        size *= 2
    return ts


def _bf16_terms(x):
    hi = x.astype(BF)
    rest = x - hi.astype(F32)
    mid = rest.astype(BF)
    lo = (rest - mid.astype(F32)).astype(BF)
    return hi, mid, lo


def _scan_kernel(*refs, steps, chunks, has_init, emit_state):
    xf_ref, xb_ref, gf_ref, gb_ref = refs[:4]
    rest = list(refs[4:])
    s0_refs = (rest.pop(0), rest.pop(0)) if has_init else None
    of_ref, ob_ref = rest.pop(0), rest.pop(0)
    out_state_refs = (rest.pop(0), rest.pop(0)) if emit_state else None
    state, = rest
    c = pl.program_id(1)

    @pl.when(c == 0)
    def _():
        if has_init:
            state[0:H_A] = s0_refs[0][0]
            state[H_A:2 * H_A] = s0_refs[1][0]
        else:
            state[...] = jnp.zeros(state.shape, F32)

    row = lax.broadcasted_iota(jnp.int32, (CHUNK, CHUNK), 0)
    col = lax.broadcasted_iota(jnp.int32, (CHUNK, CHUNK), 1)
    x_refs, g_refs, o_refs = (xf_ref, xb_ref), (gf_ref, gb_ref), (of_ref, ob_ref)
    incl = (row >= col, row <= col)
    strict = (row > col, row < col)
    tri = [jnp.where(m, 1.0, 0.0).astype(BF) for m in incl]
    chains = [(d, hh) for d in range(2) for hh in range(H_A)]
    nch = len(chains)

    def one_chunk(sub, carry):
        rows = (pl.ds(pl.multiple_of(sub * CHUNK, CHUNK), CHUNK),
                pl.ds(pl.multiple_of((chunks - 1 - sub) * CHUNK, CHUNK), CHUNK))
        gbeta = [g_refs[d][rows[d], :] for d in range(2)]
        gc = [sum(jnp.dot(tri[d], part, preferred_element_type=F32) for part in _bf16_terms(gbeta[d]))
              for d in range(2)]
        gct = [g.T for g in gc]
        q = [x_refs[d][rows[d], hh * DK_A:(hh + 1) * DK_A] for d, hh in chains]
        k = [x_refs[d][rows[d], (H_A + hh) * DK_A:(H_A + hh + 1) * DK_A] for d, hh in chains]
        v = [x_refs[d][rows[d], (2 * H_A + hh) * DK_A:(2 * H_A + hh + 1) * DK_A] for d, hh in chains]
        gcol = [gc[d][:, d * H_A + hh:d * H_A + hh + 1] for d, hh in chains]
        beta = [gbeta[d][:, DEC_A + d * H_A + hh:DEC_A + d * H_A + hh + 1] for d, hh in chains]
        decay = [jnp.exp(jnp.where(incl[d], gcol[i] - gct[d][d * H_A + hh:d * H_A + hh + 1, :], NEG))
                 for i, (d, hh) in enumerate(chains)]
        qk_kk = [_dot_nt(jnp.concatenate([q[i].astype(BF), k[i].astype(BF)], axis=0), k[i]) for i in range(nch)]
        tmat = _unit_tri_inverses([jnp.where(strict[d], qk_kk[i][CHUNK:] * beta[i] * decay[i], 0.0)
                                   for i, (d, hh) in enumerate(chains)])
        egc = [jnp.exp(g) for g in gcol]
        uw = [_dot(tmat[i], jnp.concatenate([v[i] * beta[i], k[i] * (beta[i] * egc[i])], axis=1))
              for i in range(nch)]
        g_last = [gcol[i][CHUNK - 1:CHUNK] if d == 0 else gcol[i][0:1] for i, (d, hh) in enumerate(chains)]
        s_old = [state[i] for i in range(nch)]
        s_bf = [s.astype(BF) for s in s_old]
        ws_qs = [_dot(jnp.concatenate([uw[i][:, DV_A:].astype(BF), (q[i] * egc[i]).astype(BF)], axis=0), s_bf[i])
                 for i in range(nch)]
        v_new = [uw[i][:, :DV_A] - ws_qs[i][:CHUNK] for i in range(nch)]
        kd_t = [(k[i] * jnp.exp(g_last[i] - gcol[i])).T for i in range(nch)]
        av_kv = [_dot(jnp.concatenate([(qk_kk[i][:CHUNK] * decay[i]).astype(BF), kd_t[i].astype(BF)], axis=0),
                      v_new[i]) for i in range(nch)]
        for i, (d, hh) in enumerate(chains):
            o_refs[d][rows[d], hh * DV_A:(hh + 1) * DV_A] = ws_qs[i][CHUNK:] + av_kv[i][:CHUNK]
        for i in range(nch):
            state[i] = s_old[i] * jnp.exp(g_last[i]) + av_kv[i][CHUNK:]
        return carry

    lax.fori_loop(0, chunks, one_chunk, 0, unroll=True)

    if emit_state:
        @pl.when(c == steps - 1)
        def _():
            out_state_refs[0][0] = state[0:H_A]
            out_state_refs[1][0] = state[H_A:2 * H_A]


def _delta_scan(qkvc, gbeta, init, nb, seq, emit_state):
    n = qkvc.shape[0]
    chunks = min(SCAN_CHUNKS_PER_STEP, seq // CHUNK)
    rows = chunks * CHUNK
    steps = seq // rows
    fwd = lambda w: pl.BlockSpec((rows, w), lambda b, c: (b * steps + c, 0))
    bwd = lambda w: pl.BlockSpec((rows, w), lambda b, c: (b * steps + steps - 1 - c, 0))
    st = pl.BlockSpec((1, H_A, DK_A, DV_A), lambda b, c: (b, 0, 0, 0))
    st_shape = jax.ShapeDtypeStruct((nb, H_A, DK_A, DV_A), F32)
    o_shape = jax.ShapeDtypeStruct((n, Z_A), F32)
    has_init = init is not None
    return pl.pallas_call(
        functools.partial(_scan_kernel, steps=steps, chunks=chunks, has_init=has_init, emit_state=emit_state),
        grid=(nb, steps),
        in_specs=[fwd(QKV_A), bwd(QKV_A), fwd(LANES), bwd(LANES)] + ([st, st] if has_init else []),
        out_specs=[fwd(Z_A), bwd(Z_A)] + ([st, st] if emit_state else []),
        out_shape=[o_shape, o_shape] + ([st_shape, st_shape] if emit_state else []),
        scratch_shapes=[pltpu.VMEM((2 * H_A, DK_A, DV_A), F32)],
        compiler_params=pltpu.CompilerParams(dimension_semantics=("parallel", "arbitrary")),
        name="delta_scan",
    )(qkvc, qkvc, gbeta, gbeta, *(init if has_init else ()))


LOG2E = float(np.log2(np.e))
ATTN_SCALE = HD_B ** -0.5 * LOG2E


def _dup_keys(keys):
    klane = lax.broadcasted_iota(jnp.int32, keys.shape, 1)
    krot = pltpu.roll(keys, HD_B, 1)
    return [jnp.where(klane < HD_B, keys, krot).astype(BF), jnp.where(klane < HD_B, krot, keys).astype(BF)]


def _attn_core(q, kdup, v_bf, valid, sink_ref, write):
    nq = q.shape[0]
    group = H_B // KV_B
    qlow = lax.broadcasted_iota(jnp.int32, (nq, LANES), 1) < HD_B
    scores = []
    for head in range(H_B):
        qt = q[:, (head // 2) * LANES:(head // 2 + 1) * LANES]
        qh = jnp.where(qlow, qt, 0.0) if head % 2 == 0 else jnp.where(qlow, 0.0, qt)
        scores.append(_dot_nt(qh, kdup[head // group]))
    probs, inv_den = [], []
    for head, s in enumerate(scores):
        if valid is not None:
            s = s + valid
        snk = sink_ref[head] * LOG2E
        m = jnp.maximum(jnp.max(s, axis=-1, keepdims=True), snk)
        p = jnp.exp2(s - m)
        inv_den.append(1.0 / (jnp.sum(p, axis=-1, keepdims=True) + jnp.exp2(snk - m)))
        probs.append(p.astype(BF))
    o_all = [jnp.dot(jnp.concatenate(probs[g * group:(g + 1) * group], axis=0), v_bf,
                     preferred_element_type=F32) for g in range(KV_B)]
    for g in range(KV_B):
        for t in range(group // 2):
            a = o_all[g][(2 * t) * nq:(2 * t + 1) * nq] * inv_den[g * group + 2 * t]
            b = o_all[g][(2 * t + 1) * nq:(2 * t + 2) * nq] * inv_den[g * group + 2 * t + 1]
            if g == 0:
                tile = jnp.where(qlow, a, pltpu.roll(b, HD_B, 1))
            else:
                tile = jnp.where(qlow, pltpu.roll(a, HD_B, 1), b)
            write((g * group // 2 + t) * LANES, tile)


def _ctx_attn_kernel(sink_ref, q_ref, k_ref, v_ref, o_ref):
    def write(col, tile):
        o_ref[:, col:col + LANES] = tile

    _attn_core(q_ref[...] * ATTN_SCALE, _dup_keys(k_ref[...]), v_ref[...].astype(BF), None, sink_ref, write)


def _context_attention(q, k, v, sink, nb, seq):
    n = q.shape[0]
    blk = lambda w: pl.BlockSpec((seq, w), lambda b: (b, 0))
    return pl.pallas_call(
        _ctx_attn_kernel,
        grid=(nb,),
        in_specs=[pl.BlockSpec(memory_space=pltpu.SMEM), blk(Q_B), blk(K_B), blk(V_B)],
        out_specs=blk(Q_B),
        out_shape=jax.ShapeDtypeStruct((n, Q_B), F32),
        compiler_params=pltpu.CompilerParams(dimension_semantics=("parallel",)),
        name="context_attention",
    )(sink, q, k, v)


def _win_attn_kernel(sink_ref, q_ref, k_ref, v_ref, kx_ref, vx_ref, o_ref, kd_ref, vb_ref, kxd_ref, vxb_ref):
    step = pl.program_id(1)
    seq = k_ref.shape[0]
    n_local = 3 * WINDOW
    nk = n_local + kx_ref.shape[1]
    r = lax.broadcasted_iota(jnp.int32, (WINDOW, nk), 0)
    j = lax.broadcasted_iota(jnp.int32, (WINDOW, nk), 1)

    def band_bias(off):
        dist = off + r - j
        return jnp.where(((dist <= WINDOW) & (dist >= -WINDOW)) | (j >= n_local), 0.0, NEG)

    band = [band_bias(off) for off in (0, WINDOW, 2 * WINDOW)]

    @pl.when(step == 0)
    def _():
        for g, kd in enumerate(_dup_keys(k_ref[...])):
            kd_ref[g] = kd
        for g, kd in enumerate(_dup_keys(kx_ref[0])):
            kxd_ref[g] = kd
        vb_ref[...] = v_ref[...].astype(BF)
        vxb_ref[...] = vx_ref[0].astype(BF)

    def block(sub, carry):
        i = step * WIN_SUB + sub
        start = pl.multiple_of(jnp.clip((i - 1) * WINDOW, 0, seq - n_local), WINDOW)
        kdup = [jnp.concatenate([kd_ref[g, pl.ds(start, n_local), :], kxd_ref[g]], axis=0) for g in range(KV_B)]
        v_bf = jnp.concatenate([vb_ref[pl.ds(start, n_local), :], vxb_ref[...]], axis=0)
        off = i * WINDOW - start
        valid = jnp.where(off == WINDOW, band[1], jnp.where(off == 0, band[0], band[2]))
        rows = pl.ds(pl.multiple_of(sub * WINDOW, WINDOW), WINDOW)

        def write(col, tile):
            o_ref[rows, col:col + LANES] = tile

        _attn_core(q_ref[rows, :] * ATTN_SCALE, kdup, v_bf, valid, sink_ref, write)
        return carry

    lax.fori_loop(0, WIN_SUB, block, 0, unroll=True)


def _window_attention(q, k, v, k_ctx, v_ctx, sink, nb, seq):
    n = q.shape[0]
    tq = WIN_SUB * WINDOW
    steps = seq // tq
    n_ctx = k_ctx.shape[1]
    qblk = pl.BlockSpec((tq, Q_B), lambda b, i: (b * steps + i, 0))
    kv = pl.BlockSpec((seq, K_B), lambda b, i: (b, 0))
    ctx = pl.BlockSpec((1, n_ctx, K_B), lambda b, i: (b, 0, 0))
    return pl.pallas_call(
        _win_attn_kernel,
        grid=(nb, steps),
        in_specs=[pl.BlockSpec(memory_space=pltpu.SMEM), qblk, kv, kv, ctx, ctx],
        out_specs=qblk,
        out_shape=jax.ShapeDtypeStruct((n, Q_B), F32),
        scratch_shapes=[pltpu.VMEM((KV_B, seq, K_B), BF), pltpu.VMEM((seq, V_B), BF),
                        pltpu.VMEM((KV_B, n_ctx, K_B), BF), pltpu.VMEM((n_ctx, V_B), BF)],
        compiler_params=pltpu.CompilerParams(dimension_semantics=("parallel", "arbitrary")),
        name="window_attention",
    )(sink, q, k, v, k_ctx, v_ctx)


def _merge_kernel(x_ref, of_ref, ob_ref, yb_ref, sh_ref, sc_ref, g2_ref, nw_ref, on_ref,
                  wzg_ref, woa_ref, wob_ref, wout_ref, o_ref):
    x = x_ref[...]
    h = _modulated_norm(x, nw_ref, sh_ref, sc_ref)
    z = jnp.dot(h, wzg_ref[:, :Z_A], preferred_element_type=F32)
    o = of_ref[...] + ob_ref[...]
    heads = []
    for hh in range(H_A):
        sl = slice(hh * DV_A, (hh + 1) * DV_A)
        heads.append(_rms(o[:, sl], on_ref[...]) * _silu(z[:, sl]))
    y_a = jnp.concatenate(heads, axis=1)
    gate_a = jnp.dot(h, wzg_ref[:, Z_A:Z_A + D_MODEL], preferred_element_type=F32)
    gate_b = jnp.dot(h, wzg_ref[:, Z_A + D_MODEL:], preferred_element_type=F32)
    m = _sigmoid(gate_a) * _dot(y_a, woa_ref[...]) + _sigmoid(gate_b) * _dot(yb_ref[...], wob_ref[...])
    o_ref[...] = x + g2_ref[0, 0] * _dot(m, wout_ref[...])


def _merge(x, o_f, o_b, y_b, mod4, tokens_per_row, row0, norm_w, onorm, w_zg, w_oa, w_ob, w_out):
    n = x.shape[0]
    tm = MERGE_TILE
    tok = lambda w: pl.BlockSpec((tm, w), lambda i: (i, 0))
    return pl.pallas_call(
        _merge_kernel,
        grid=(n // tm,),
        in_specs=[tok(D_MODEL), tok(Z_A), tok(Z_A), tok(Q_B),
                  _mod_spec(3, tm, tokens_per_row, row0),
                  _mod_spec(4, tm, tokens_per_row, row0),
                  _mod_spec(5, tm, tokens_per_row, row0),
                  _const_spec((1, D_MODEL)),
                  _const_spec((1, DV_A)),
                  _const_spec((D_MODEL, Z_A + GATES)),
                  _const_spec((H_A * DV_A, D_MODEL)),
                  _const_spec((H_B * HD_B, D_MODEL)),
                  _const_spec((D_MODEL, D_MODEL))],
        out_specs=tok(D_MODEL),
        out_shape=jax.ShapeDtypeStruct((n, D_MODEL), F32),
        compiler_params=pltpu.CompilerParams(dimension_semantics=("parallel",),
                                             vmem_limit_bytes=VMEM_BIG),
        name="merge",
    )(x, o_f, o_b, y_b, mod4, mod4, mod4, norm_w, onorm, w_zg, w_oa, w_ob, w_out)


def _rope_tables(seq):
    rows = seq // GRID_W
    row = jnp.repeat(jnp.arange(rows, dtype=F32), GRID_W)
    col = jnp.tile(jnp.arange(GRID_W, dtype=F32), rows)
    inv = jnp.power(ROPE_BASE, -jnp.arange(ROPE_AXIS_PAIRS, dtype=F32) / ROPE_AXIS_PAIRS)
    ang = jnp.concatenate([row[:, None] * inv, col[:, None] * inv], axis=-1)
    cos, sin = jnp.cos(ang), jnp.sin(ang)
    cos_t = jnp.tile(cos, (1, LANES // (HD_B // 2)))
    sin_t = jnp.tile(jnp.concatenate([-sin, sin], axis=-1), (1, LANES // HD_B))
    return cos_t, sin_t


def _split_w_in(w_in):
    edges = np.cumsum((0, QKV_A, Z_A, DEC_A, BETA_A, Q_B, K_B, V_B, GATES))
    part = lambda i: w_in[:, int(edges[i]):int(edges[i + 1])].astype(BF)
    pad = jnp.zeros((D_MODEL, LANES - DEC_A - BETA_A), BF)
    w_mix = jnp.concatenate([part(0), part(4), part(5), part(6), part(2), part(3), pad], axis=1)
    w_zg = jnp.concatenate([part(1), part(7)], axis=1)
    return w_mix, w_zg


def _cache_layout(t, nb, seq):
    return t.reshape(nb, KV_B, HD_B, seq).transpose(0, 3, 1, 2)[:, None]


def kernel(x_prompt, x_sample, state_delta_fwd, state_delta_bwd, cache_k, cache_v, c, c_ctx, ada_w, ada_b, norm_ffn1, ffn1_w13, ffn1_w2, norm_mix, w_in, conv_w, a_log, dt_bias, onorm_a, w_oa, w_ob, w_out, sink, norm_ffn2, ffn2_w13, ffn2_w2, norm_final):
    bp, sp = x_prompt.shape[:2]
    bs, ts = x_sample.shape[:2]
    layer = 0

    cvec = jnp.concatenate([c_ctx[None, :], c, jnp.zeros((MOD_ROWS - 1 - bs, D_MODEL), F32)], axis=0)
    mod4 = _modulation(cvec, ada_w[layer], ada_b[layer]).reshape(MOD_ROWS, N_MOD, 1, D_MODEL)

    row = lambda a: a.reshape(1, -1)
    lane_pad = lambda a: jnp.pad(a.reshape(1, -1), ((0, 0), (0, LANES - a.size)))
    w13_1, w2_1 = ffn1_w13[layer].astype(BF), ffn1_w2[layer].astype(BF)
    w13_2, w2_2 = ffn2_w13[layer].astype(BF), ffn2_w2[layer].astype(BF)
    w_mix, w_zg = _split_w_in(w_in[layer])
    w_oa_b, w_ob_b, w_out_b = w_oa[layer].astype(BF), w_ob[layer].astype(BF), w_out[layer].astype(BF)
    alog_p, dtb_p = lane_pad(a_log[layer]), lane_pad(dt_bias[layer])
    cos_t, sin_t = _rope_tables(ts)
    final_w = row(norm_final)

    def run(x, nb, seq, tokens_per_row, row0, init, ctx):
        latent = ctx is not None
        x1 = _ffn(x, mod4, (0, 1, 2), tokens_per_row, row0, row(norm_ffn1[layer]), w13_1, w2_1,
                  final_w, final=False)
        mix = _inproj(x1, mod4, tokens_per_row, row0, row(norm_mix[layer]), w_mix, cos_t, sin_t,
                      conv_w[layer], alog_p, dtb_p, seq, latent)
        qkvc, qb, kb, vb, gbeta = mix[:5]
        scan = _delta_scan(qkvc, gbeta, init, nb, seq, emit_state=not latent)
        if latent:
            y_b = _window_attention(qb, kb, vb, ctx[0], ctx[1], sink[layer], nb, seq)
        else:
            y_b = _context_attention(qb, kb, vb, sink[layer], nb, seq)
        x2 = _merge(x1, scan[0], scan[1], y_b, mod4, tokens_per_row, row0, row(norm_mix[layer]),
                    row(onorm_a[layer]), w_zg, w_oa_b, w_ob_b, w_out_b)
        y = _ffn(x2, mod4, (6, 7, 8), tokens_per_row, row0, row(norm_ffn2[layer]), w13_2, w2_2,
                 final_w, final=True)
        return y, scan[2:], mix[5:]

    yp, (s_f, s_b), (k_t, v_t) = run(x_prompt.reshape(bp * sp, D_MODEL), bp, sp, bp * sp, 0, None, None)
    ctx = (cache_k[:, layer].reshape(bs, -1, K_B), cache_v[:, layer].reshape(bs, -1, V_B))
    ys, _, _ = run(x_sample.reshape(bs * ts, D_MODEL), bs, ts, ts, 1,
                   (state_delta_fwd[:, layer], state_delta_bwd[:, layer]), ctx)

    return (yp.reshape(bp, sp, D_MODEL), ys.reshape(bs, ts, D_MODEL),
            s_f[:, None], s_b[:, None], _cache_layout(k_t, bp, sp), _cache_layout(v_t, bp, sp))
```
